```python
import math
import jax, jax.numpy as jnp
from jax import lax
import numpy as np

D_MODEL = 1024
BATCH = 2
SEQ = 16384
DEPTH = 4

N_MIXERS = 3
RET_HEADS = 4
RET_QK_DIM = D_MODEL // RET_HEADS
RET_V_DIM = 2 * D_MODEL // RET_HEADS
RET_CHUNK = 256
S5_WIDTH = D_MODEL
S5_GROUP = 16
S5_GROUPS = S5_WIDTH // S5_GROUP
S5_STATE = 64
S5_CHUNK = 1024
S5_DT_MIN = 1e-3
S5_DT_MAX = 1e-1
MOBA_HEADS = 8
MOBA_HEAD_DIM = D_MODEL // MOBA_HEADS
MOBA_BLOCK = 256
MOBA_TOPK = 3
MOBA_Q_CHUNK = 32
D_FF = 3584
N_EXPERTS = 8
TOP_K = 2
MOE_BLOCK = 256
NORM_EPS = 1e-6
NEG_INF = -1e30

kernel_name = 'hybrid_retention_s5_moba_moe_trunk'


def _layer_counts(depth):
    n_ret = sum(1 for i in range(depth) if i % N_MIXERS == 0)
    n_s5 = sum(1 for i in range(depth) if i % N_MIXERS == 1)
    n_moba = sum(1 for i in range(depth) if i % N_MIXERS == 2)
    n_dense = sum(1 for i in range(depth) if i % 2 == 0)
    n_moe = depth - n_dense
    return n_ret, n_s5, n_moba, n_dense, n_moe


def _rmsnorm(x, gain):
    xf = x.astype(jnp.float32)
    y = xf * lax.rsqrt(jnp.mean(xf * xf, axis=-1, keepdims=True) + NORM_EPS)
    return (y * gain.astype(jnp.float32)).astype(x.dtype)


def _swiglu(h, w_gate, w_up, w_down):
    return (jax.nn.silu(h @ w_gate) * (h @ w_up)) @ w_down


def _retention(h, w_in, gn_gain, w_out):
    B, T, _ = h.shape
    H, dk, dv = RET_HEADS, RET_QK_DIM, RET_V_DIM
    C = math.gcd(T, RET_CHUNK)
    nC = T // C
    proj = h @ w_in
    q, k, v, g = jnp.split(proj, [D_MODEL, 2 * D_MODEL, 4 * D_MODEL], axis=-1)

    def heads(a, d):
        return a.reshape(B, nC, C, H, d).transpose(0, 3, 1, 2, 4).astype(jnp.float32)

    q = heads(q, dk)
    k = heads(k, dk) * (dk ** -0.5)
    v = heads(v, dv)
    log_gamma = jnp.log1p(-jnp.exp2(-5.0 - jnp.arange(H, dtype=jnp.float32)))
    idx = jnp.arange(C, dtype=jnp.float32)
    diff = idx[:, None] - idx[None, :]
    decay = jnp.where(diff >= 0, jnp.exp(log_gamma[:, None, None] * jnp.maximum(diff, 0.0)), 0.0)
    scores = jnp.einsum('bhncd,bhned->bhnce', q, k) * decay[:, None]
    y_intra = jnp.einsum('bhnce,bhnev->bhncv', scores, v)
    q_decay = jnp.exp(log_gamma[:, None] * (idx + 1.0))
    k_decay = jnp.exp(log_gamma[:, None] * (C - 1.0 - idx))
    chunk_decay = jnp.exp(log_gamma * C)

    def step(R, inp):
        qc, kc, vc = inp
        y = jnp.einsum('bhcd,bhdv->bhcv', qc, R) * q_decay[None, :, :, None]
        R = R * chunk_decay[None, :, None, None] + jnp.einsum(
            'bhcd,bhcv->bhdv', kc * k_decay[None, :, :, None], vc)
        return R, y

    R0 = jnp.zeros((B, H, dk, dv), jnp.float32)
    xs = (q.transpose(2, 0, 1, 3, 4), k.transpose(2, 0, 1, 3, 4), v.transpose(2, 0, 1, 3, 4))
    _, y_cross = lax.scan(step, R0, xs)
    y = y_intra + y_cross.transpose(1, 2, 0, 3, 4)
    mu = jnp.mean(y, axis=-1, keepdims=True)
    var = jnp.mean(jnp.square(y - mu), axis=-1, keepdims=True)
    y = (y - mu) * lax.rsqrt(var + NORM_EPS)
    y = y.transpose(0, 2, 3, 1, 4).reshape(B, T, H * dv) * gn_gain.astype(jnp.float32)
    y = y * jax.nn.silu(g.astype(jnp.float32))
    return y.astype(h.dtype) @ w_out


def _s5(h, w_in, a_re, a_im, log_step, b_re, b_im, c_re, c_im, d_skip, w_out):
    B, T, _ = h.shape
    G, P, N = S5_GROUPS, S5_STATE, S5_GROUP
    f32 = jnp.float32
    u = (h @ w_in).astype(f32)
    dt = jnp.exp(log_step.astype(f32))[:, None]
    ar, ai = a_re.astype(f32), a_im.astype(f32)
    mag = jnp.exp(ar * dt)
    abar_re = mag * jnp.cos(ai * dt)
    abar_im = mag * jnp.sin(ai * dt)
    den = ar * ar + ai * ai
    nr = abar_re - 1.0
    ni = abar_im
    f_re = (nr * ar + ni * ai) / den
    f_im = (ni * ar - nr * ai) / den
    br, bi = b_re.astype(f32), b_im.astype(f32)
    bbar_re = f_re[..., None] * br - f_im[..., None] * bi
    bbar_im = f_re[..., None] * bi + f_im[..., None] * br
    cr, ci = c_re.astype(f32), c_im.astype(f32)
    L = math.gcd(T, S5_CHUNK)
    nC = T // L
    u_chunks = u.reshape(B, nC, L, G, N).transpose(1, 2, 0, 3, 4)

    def combine(e1, e2):
        a1r, a1i, b1r, b1i = e1
        a2r, a2i, b2r, b2i = e2
        return (a2r * a1r - a2i * a1i, a2r * a1i + a2i * a1r,
                a2r * b1r - a2i * b1i + b2r, a2r * b1i + a2i * b1r + b2i)

    def step(carry, uc):
        sr, si = carry
        bu_re = jnp.einsum('lbgn,gpn->lbgp', uc, bbar_re)
        bu_im = jnp.einsum('lbgn,gpn->lbgp', uc, bbar_im)
        a_r = jnp.broadcast_to(abar_re[None, None], bu_re.shape)
        a_i = jnp.broadcast_to(abar_im[None, None], bu_re.shape)
        Ar, Ai, Hr, Hi = lax.associative_scan(combine, (a_r, a_i, bu_re, bu_im), axis=0)
        hr = Hr + Ar * sr - Ai * si
        hi = Hi + Ar * si + Ai * sr
        y = jnp.einsum('lbgp,gnp->lbgn', hr, cr) - jnp.einsum('lbgp,gnp->lbgn', hi, ci)
        return (hr[-1], hi[-1]), y

    s0 = (jnp.zeros((B, G, P), f32), jnp.zeros((B, G, P), f32))
    _, y = lax.scan(step, s0, u_chunks)
    y = y.transpose(2, 0, 1, 3, 4).reshape(B, T, S5_WIDTH) + d_skip.astype(f32) * u
    y = jax.nn.gelu(y)
    z = y.astype(h.dtype) @ w_out
    za, zb = jnp.split(z, 2, axis=-1)
    return za * jax.nn.sigmoid(zb)


def _moba(h, w_in, w_out):
    B, T, _ = h.shape
    H, hd, Bk = MOBA_HEADS, MOBA_HEAD_DIM, MOBA_BLOCK
    f32 = jnp.float32
    pad = (-T) % Bk
    Tp = T + pad
    nb = Tp // Bk
    K = max(1, min(MOBA_TOPK, nb - 1))
    q, k, v = jnp.split(h @ w_in, 3, axis=-1)

    def heads(a):
        a = jnp.pad(a, ((0, 0), (0, pad), (0, 0)))
        return a.reshape(B, Tp, H, hd).transpose(0, 2, 1, 3)

    q = heads(q) * (hd ** -0.5)
    k = heads(k)
    v = heads(v)
    k_blk = k.reshape(B, H, nb, Bk, hd)
    v_blk = v.reshape(B, H, nb, Bk, hd)
    k_mean = jnp.mean(k_blk.astype(f32), axis=3)
    slopes = jnp.exp2(-8.0 * (jnp.arange(H, dtype=f32) + 1.0) / H)
    kpos = jnp.arange(Bk)
    blk_ids = jnp.arange(nb)
    bidx = jnp.arange(B)[:, None, None, None]
    hidx = jnp.arange(H)[None, :, None, None]
    Qc = MOBA_Q_CHUNK

    def chunk(ci):
        t0 = ci * Qc
        qc = lax.dynamic_slice_in_dim(q, t0, Qc, axis=2)
        qpos = t0 + jnp.arange(Qc)
        own = t0 // Bk
        gate = jnp.einsum('bhqd,bhnd->bhqn', qc.astype(f32), k_mean)
        gate = jnp.where(blk_ids < own, gate, NEG_INF)
        _, sel = lax.top_k(gate, K)
        sel_valid = jnp.arange(K) < own
        k_sel = k_blk[bidx, hidx, sel]
        v_sel = v_blk[bidx, hidx, sel]
        pos_sel = sel[..., None] * Bk + kpos
        s_sel = jnp.einsum('bhqd,bhqkjd->bhqkj', qc, k_sel).astype(f32)
        s_sel = s_sel - slopes[None, :, None, None, None] * (qpos[:, None, None] - pos_sel).astype(f32)
        s_sel = jnp.where(sel_valid[:, None], s_sel, NEG_INF).reshape(B, H, Qc, K * Bk)
        k_own = lax.dynamic_slice_in_dim(k, own * Bk, Bk, axis=2)
        v_own = lax.dynamic_slice_in_dim(v, own * Bk, Bk, axis=2)
        own_pos = own * Bk + kpos
        dist = (qpos[:, None] - own_pos[None, :]).astype(f32)
        s_own = jnp.einsum('bhqd,bhjd->bhqj', qc, k_own).astype(f32) - slopes[:, None, None] * dist
        s_own = jnp.where(dist >= 0, s_own, NEG_INF)
        p = jax.nn.softmax(jnp.concatenate([s_own, s_sel], axis=-1), axis=-1)
        p_own = p[..., :Bk].astype(v.dtype)
        p_sel = p[..., Bk:].reshape(B, H, Qc, K, Bk).astype(v.dtype)
        return (jnp.einsum('bhqj,bhjd->bhqd', p_own, v_own)
                + jnp.einsum('bhqkj,bhqkjd->bhqd', p_sel, v_sel))

    o = lax.map(chunk, jnp.arange(Tp // Qc))
    o = o.transpose(1, 0, 3, 2, 4).reshape(B, Tp, H * hd)[:, :T]
    return o @ w_out


def _moe(h, router, w_gate, w_up, w_down):
    B, T, D = h.shape
    xf = h.reshape(-1, D)
    N = xf.shape[0]
    logits = (xf @ router).astype(jnp.float32)
    top_logits, top_idx = lax.top_k(logits, TOP_K)
    gates = jax.nn.softmax(top_logits, axis=-1)
    e_flat = top_idx.reshape(-1)
    tok_flat = jnp.arange(N * TOP_K, dtype=jnp.int32) // TOP_K
    g_flat = gates.reshape(-1)
    order = jnp.argsort(e_flat)
    e_sorted = e_flat[order]
    counts = jnp.bincount(e_flat, length=N_EXPERTS)
    padded = ((counts + MOE_BLOCK - 1) // MOE_BLOCK) * MOE_BLOCK
    start = jnp.cumsum(counts) - counts
    pend = jnp.cumsum(padded)
    pstart = pend - padded
    rank = jnp.arange(N * TOP_K, dtype=jnp.int32) - start[e_sorted]
    dest = pstart[e_sorted] + rank
    n_slots = (-(-(N * TOP_K) // MOE_BLOCK)) * MOE_BLOCK + N_EXPERTS * MOE_BLOCK
    n_blocks = n_slots // MOE_BLOCK
    slot_tok = jnp.full((n_slots,), N, jnp.int32).at[dest].set(tok_flat[order])
    slot_gate = jnp.zeros((n_slots,), jnp.float32).at[dest].set(g_flat[order])
    block_e = jnp.minimum(
        jnp.searchsorted(pend, jnp.arange(n_blocks) * MOE_BLOCK, side='right'), N_EXPERTS - 1)
    x_pad = jnp.concatenate([xf, jnp.zeros((1, D), xf.dtype)], axis=0)

    def run_block(args):
        tok, e = args
        xb = x_pad[tok]
        return _swiglu(xb, w_gate[e], w_up[e], w_down[e])

    y = lax.map(run_block, (slot_tok.reshape(n_blocks, MOE_BLOCK), block_e))
    y = y.reshape(n_slots, D) * slot_gate[:, None].astype(y.dtype)
    out = jax.ops.segment_sum(y, slot_tok, num_segments=N + 1)[:N]
    return out.reshape(B, T, D)


def setup_inputs(seed: int = 0) -> dict:
    key = jax.random.key(seed)
    ks = iter(jax.random.split(key, 32))
    f32 = jnp.float32

    def nrm(shape, scale):
        return jax.random.normal(next(ks), shape, f32) * scale

    D, F, E = D_MODEL, D_FF, N_EXPERTS
    G, P, N = S5_GROUPS, S5_STATE, S5_GROUP
    NR, NS, NM, ND, NE = _layer_counts(DEPTH)
    out_scale = (2.0 * DEPTH) ** -0.5
    x = nrm((BATCH, SEQ, D), 1.0)
    norm_mix = 1.0 + nrm((DEPTH, D), 0.02)
    norm_ffn = 1.0 + nrm((DEPTH, D), 0.02)
    norm_final = 1.0 + nrm((D,), 0.02)
    ret_w_in = nrm((NR, D, 6 * D), D ** -0.5)
    ret_gn = 1.0 + nrm((NR, 2 * D), 0.02)
    ret_w_out = nrm((NR, 2 * D, D), (2 * D) ** -0.5 * out_scale)
    s5_w_in = nrm((NS, D, S5_WIDTH), D ** -0.5)
    s5_a_re = -0.5 * jnp.exp(nrm((NS, G, P), 0.05))
    s5_a_im = math.pi * jnp.arange(P, dtype=f32) + nrm((NS, G, P), 0.01)
    s5_log_step = math.log(S5_DT_MIN) + jax.random.uniform(next(ks), (NS, G), f32) * (
        math.log(S5_DT_MAX) - math.log(S5_DT_MIN))
    s5_b_re = nrm((NS, G, P, N), (2 * N) ** -0.5)
    s5_b_im = nrm((NS, G, P, N), (2 * N) ** -0.5)
    s5_c_re = nrm((NS, G, N, P), (2 * P) ** -0.5)
    s5_c_im = nrm((NS, G, N, P), (2 * P) ** -0.5)
    s5_d = nrm((NS, S5_WIDTH), 1.0)
    s5_w_out = nrm((NS, S5_WIDTH, 2 * D), S5_WIDTH ** -0.5 * out_scale)
    moba_w_in = nrm((NM, D, 3 * D), D ** -0.5)
    moba_w_out = nrm((NM, D, D), D ** -0.5 * out_scale)
    ffn_w_gate = nrm((ND, D, F), D ** -0.5)
    ffn_w_up = nrm((ND, D, F), D ** -0.5)
    ffn_w_down = nrm((ND, F, D), F ** -0.5 * out_scale)
    moe_router = nrm((NE, D, E), D ** -0.5)
    moe_w_gate = nrm((NE, E, D, F), D ** -0.5)
    moe_w_up = nrm((NE, E, D, F), D ** -0.5)
    moe_w_down = nrm((NE, E, F, D), F ** -0.5 * out_scale)
    return {'x': x, 'norm_mix': norm_mix, 'norm_ffn': norm_ffn, 'norm_final': norm_final,
            'ret_w_in': ret_w_in, 'ret_gn': ret_gn, 'ret_w_out': ret_w_out,
            's5_w_in': s5_w_in, 's5_a_re': s5_a_re, 's5_a_im': s5_a_im, 's5_log_step': s5_log_step,
            's5_b_re': s5_b_re, 's5_b_im': s5_b_im, 's5_c_re': s5_c_re, 's5_c_im': s5_c_im,
            's5_d': s5_d, 's5_w_out': s5_w_out,
            'moba_w_in': moba_w_in, 'moba_w_out': moba_w_out,
            'ffn_w_gate': ffn_w_gate, 'ffn_w_up': ffn_w_up, 'ffn_w_down': ffn_w_down,
            'moe_router': moe_router, 'moe_w_gate': moe_w_gate, 'moe_w_up': moe_w_up,
            'moe_w_down': moe_w_down}


def reference(x, norm_mix, norm_ffn, norm_final, ret_w_in, ret_gn, ret_w_out,
              s5_w_in, s5_a_re, s5_a_im, s5_log_step, s5_b_re, s5_b_im, s5_c_re, s5_c_im,
              s5_d, s5_w_out, moba_w_in, moba_w_out, ffn_w_gate, ffn_w_up, ffn_w_down,
              moe_router, moe_w_gate, moe_w_up, moe_w_down):
    h = x
    i_ret = i_s5 = i_moba = i_dense = i_moe = 0
    for i in range(DEPTH):
        hn = _rmsnorm(h, norm_mix[i])
        mixer = i % N_MIXERS
        if mixer == 0:
            mix = _retention(hn, ret_w_in[i_ret], ret_gn[i_ret], ret_w_out[i_ret])
            i_ret += 1
        elif mixer == 1:
            mix = _s5(hn, s5_w_in[i_s5], s5_a_re[i_s5], s5_a_im[i_s5], s5_log_step[i_s5],
                      s5_b_re[i_s5], s5_b_im[i_s5], s5_c_re[i_s5], s5_c_im[i_s5],
                      s5_d[i_s5], s5_w_out[i_s5])
            i_s5 += 1
        else:
            mix = _moba(hn, moba_w_in[i_moba], moba_w_out[i_moba])
            i_moba += 1
        h = h + mix.astype(h.dtype)
        hn = _rmsnorm(h, norm_ffn[i])
        if i % 2 == 0:
            ffn = _swiglu(hn, ffn_w_gate[i_dense], ffn_w_up[i_dense], ffn_w_down[i_dense])
            i_dense += 1
        else:
            ffn = _moe(hn, moe_router[i_moe], moe_w_gate[i_moe], moe_w_up[i_moe], moe_w_down[i_moe])
            i_moe += 1
        h = h + ffn.astype(h.dtype)
    return _rmsnorm(h, norm_final)
```

```python
import functools
import math

import numpy as np
import jax
import jax.numpy as jnp
from jax import lax
from jax.experimental import pallas as pl
from jax.experimental.pallas import tpu as pltpu

F32 = jnp.float32
BF16 = jnp.bfloat16

NORM_EPS = 1e-6
NEG_INF = -1e30
N_MIXERS = 3

RET_HEADS = 4
RET_CHUNK = 256
S5_GROUP = 16
S5_LANE_GROUPS = 8
S5_SUBCHUNK = 16
S5_ROWS = 128
S5_DT_MIN = 1e-3
MOBA_HEADS = 8
MOBA_BLOCK = 256
MOBA_TOPK = 3
TOP_K = 2
MOE_ROWS = 1024

V7X_VMEM_LIMIT_BYTES = 56 * 1024 * 1024
LANES = 128


def _params(*sem):
    return pltpu.CompilerParams(dimension_semantics=sem, vmem_limit_bytes=V7X_VMEM_LIMIT_BYTES)


def _rms(x, gain):
    return x * lax.rsqrt(jnp.mean(x * x, axis=-1, keepdims=True) + NORM_EPS) * gain


def _dot(a, b):
    return jnp.dot(a, b, preferred_element_type=F32)


def _dot_nt(a, b):
    return lax.dot_general(a, b, (((1,), (1,)), ((), ())), preferred_element_type=F32)


def _dot_tn(a, b):
    return lax.dot_general(a, b, (((0,), (0,)), ((), ())), preferred_element_type=F32)


def _norm_matmul_kernel(x_ref, g_ref, w_ref, o_ref, xn_ref):
    @pl.when(pl.program_id(1) == 0)
    def _():
        xn_ref[...] = _rms(x_ref[...], g_ref[...]).astype(BF16)

    o_ref[...] = _dot(xn_ref[...], w_ref[...]).astype(o_ref.dtype)


def norm_matmul(x, gain, w, out_dtype, tm=1024, tn=1024):
    m, d = x.shape
    n = w.shape[1]
    tm, tn = min(tm, m), min(tn, n)
    return pl.pallas_call(
        _norm_matmul_kernel,
        grid=(m // tm, n // tn),
        in_specs=[pl.BlockSpec((tm, d), lambda i, j: (i, 0)),
                  pl.BlockSpec((1, d), lambda i, j: (0, 0)),
                  pl.BlockSpec((d, tn), lambda i, j: (0, j))],
        out_specs=pl.BlockSpec((tm, tn), lambda i, j: (i, j)),
        out_shape=jax.ShapeDtypeStruct((m, n), out_dtype),
        scratch_shapes=[pltpu.VMEM((tm, d), BF16)],
        compiler_params=_params("parallel", "arbitrary"),
        name="norm_matmul",
    )(x, gain.reshape(1, d), w)


def _matmul_res_kernel(a_ref, w_ref, r_ref, o_ref):
    o_ref[...] = r_ref[...] + _dot(a_ref[...].astype(BF16), w_ref[...])


def matmul_residual(a, w, res, tm=512):
    m, k = a.shape
    n = w.shape[1]
    tm = min(tm, m)
    return pl.pallas_call(
        _matmul_res_kernel,
        grid=(m // tm,),
        in_specs=[pl.BlockSpec((tm, k), lambda i: (i, 0)),
                  pl.BlockSpec((k, n), lambda i: (0, 0)),
                  pl.BlockSpec((tm, n), lambda i: (i, 0))],
        out_specs=pl.BlockSpec((tm, n), lambda i: (i, 0)),
        out_shape=jax.ShapeDtypeStruct((m, n), F32),
        compiler_params=_params("parallel"),
        name="matmul_residual",
    )(a, w, res)


def _matmul_glu_res_kernel(a_ref, wa_ref, wb_ref, r_ref, o_ref):
    a = a_ref[...].astype(BF16)
    za = _dot(a, wa_ref[...])
    zb = _dot(a, wb_ref[...])
    o_ref[...] = r_ref[...] + za * jax.nn.sigmoid(zb)


def matmul_glu_residual(a, w, res, tm=512):
    m, k = a.shape
    n = w.shape[1] // 2
    tm = min(tm, m)
    return pl.pallas_call(
        _matmul_glu_res_kernel,
        grid=(m // tm,),
        in_specs=[pl.BlockSpec((tm, k), lambda i: (i, 0)),
                  pl.BlockSpec((k, n), lambda i: (0, 0)),
                  pl.BlockSpec((k, n), lambda i: (0, 1)),
                  pl.BlockSpec((tm, n), lambda i: (i, 0))],
        out_specs=pl.BlockSpec((tm, n), lambda i: (i, 0)),
        out_shape=jax.ShapeDtypeStruct((m, n), F32),
        compiler_params=_params("parallel"),
        name="matmul_glu_residual",
    )(a, w, w, res)


def _swiglu_step(xn, wg_ref, wu_ref, wd_ref):
    g = _dot(xn, wg_ref[...])
    u = _dot(xn, wu_ref[...])
    a = (g * jax.nn.sigmoid(g) * u).astype(BF16)
    return _dot(a, wd_ref[...])


def _ffn_kernel(x_ref, g_ref, wg_ref, wu_ref, wd_ref, o_ref, xn_ref, acc_ref):
    j = pl.program_id(1)

    @pl.when(j == 0)
    def _():
        xn_ref[...] = _rms(x_ref[...], g_ref[...]).astype(BF16)
        acc_ref[...] = jnp.zeros_like(acc_ref)

    acc_ref[...] += _swiglu_step(xn_ref[...], wg_ref, wu_ref, wd_ref)

    @pl.when(j == pl.num_programs(1) - 1)
    def _():
        o_ref[...] = x_ref[...] + acc_ref[...]


def dense_ffn(x, gain, wg, wu, wd, tm=1024, tf=512):
    m, d = x.shape
    f = wg.shape[1]
    tm, tf = min(tm, m), min(tf, f)
    return pl.pallas_call(
        _ffn_kernel,
        grid=(m // tm, f // tf),
        in_specs=[pl.BlockSpec((tm, d), lambda i, j: (i, 0)),
                  pl.BlockSpec((1, d), lambda i, j: (0, 0)),
                  pl.BlockSpec((d, tf), lambda i, j: (0, j)),
                  pl.BlockSpec((d, tf), lambda i, j: (0, j)),
                  pl.BlockSpec((tf, d), lambda i, j: (j, 0))],
        out_specs=pl.BlockSpec((tm, d), lambda i, j: (i, 0)),
        out_shape=jax.ShapeDtypeStruct((m, d), F32),
        scratch_shapes=[pltpu.VMEM((tm, d), BF16), pltpu.VMEM((tm, d), F32)],
        compiler_params=_params("parallel", "arbitrary"),
        name="dense_ffn",
    )(x, gain.reshape(1, d), wg, wu, wd)


def _retention_kernel(q_ref, k_ref, v_ref, g_ref, dec_ref, qd_ref, kd_ref, gn_ref, o_ref, r_ref,
                      *, heads, dk, dv, chunk_decay):
    @pl.when(pl.program_id(1) == 0)
    def _():
        r_ref[...] = jnp.zeros_like(r_ref)

    for h in range(heads):
        qh = q_ref[:, h * dk:(h + 1) * dk]
        kh = k_ref[:, h * dk:(h + 1) * dk]
        vh = v_ref[:, h * dv:(h + 1) * dv]
        s = _dot_nt(qh, kh) * dec_ref[h]
        y = _dot(s.astype(BF16), vh)
        y = y + _dot(qh, r_ref[h].astype(BF16)) * qd_ref[h]
        kd = (kh.astype(F32) * kd_ref[h]).astype(BF16)
        r_ref[h] = r_ref[h] * chunk_decay[h] + _dot_tn(kd, vh)
        mu = jnp.mean(y, axis=-1, keepdims=True)
        yc = y - mu
        var = jnp.mean(yc * yc, axis=-1, keepdims=True)
        yn = yc * lax.rsqrt(var + NORM_EPS) * gn_ref[:, h * dv:(h + 1) * dv]
        gate = g_ref[:, h * dv:(h + 1) * dv].astype(F32)
        o_ref[:, h * dv:(h + 1) * dv] = (yn * (gate * jax.nn.sigmoid(gate))).astype(o_ref.dtype)


def retention_core(proj, gn_gain, batch, seq):
    d = proj.shape[1] // 6
    heads = RET_HEADS
    dk, dv = d // heads, 2 * d // heads
    c = math.gcd(seq, RET_CHUNK)
    nc = seq // c
    log_gamma = np.log1p(-np.exp2(-5.0 - np.arange(heads, dtype=np.float64)))
    idx = np.arange(c, dtype=np.float64)
    diff = idx[:, None] - idx[None, :]
    scale = dk ** -0.5
    decay = np.where(diff >= 0, np.exp(log_gamma[:, None, None] * np.maximum(diff, 0.0)), 0.0) * scale
    q_decay = np.exp(log_gamma[:, None] * (idx + 1.0))[:, :, None]
    k_decay = np.exp(log_gamma[:, None] * (c - 1.0 - idx))[:, :, None] * scale
    chunk_decay = tuple(float(v) for v in np.exp(log_gamma * c))
    kern = functools.partial(_retention_kernel, heads=heads, dk=dk, dv=dv, chunk_decay=chunk_decay)
    row = lambda b, n: b * nc + n
    return pl.pallas_call(
        kern,
        grid=(batch, nc),
        in_specs=[pl.BlockSpec((c, d), lambda b, n: (row(b, n), 0)),
                  pl.BlockSpec((c, d), lambda b, n: (row(b, n), 1)),
                  pl.BlockSpec((c, 2 * d), lambda b, n: (row(b, n), 1)),
                  pl.BlockSpec((c, 2 * d), lambda b, n: (row(b, n), 2)),
                  pl.BlockSpec((heads, c, c), lambda b, n: (0, 0, 0)),
                  pl.BlockSpec((heads, c, 1), lambda b, n: (0, 0, 0)),
                  pl.BlockSpec((heads, c, 1), lambda b, n: (0, 0, 0)),
                  pl.BlockSpec((1, 2 * d), lambda b, n: (0, 0))],
        out_specs=pl.BlockSpec((c, 2 * d), lambda b, n: (row(b, n), 0)),
        out_shape=jax.ShapeDtypeStruct((batch * seq, 2 * d), BF16),
        scratch_shapes=[pltpu.VMEM((heads, dk, dv), F32)],
        compiler_params=_params("parallel", "arbitrary"),
        name="retention_core",
    )(proj, proj, proj, proj, jnp.asarray(decay, F32), jnp.asarray(q_decay, F32),
      jnp.asarray(k_decay, F32), gn_gain.reshape(1, 2 * d).astype(F32))


def _gelu_tanh(x):
    return 0.5 * x * (1.0 + jnp.tanh(math.sqrt(2.0 / math.pi) * (x + 0.044715 * (x * x * x))))


def _s5_kernel(u_ref, d_ref, m1_ref, m2_ref, al_ref, o_ref, xl_ref, xp_ref, st_ref,
               *, sub, rows, steps_per_seq):
    half = st_ref.shape[1] // 2
    width = sub * LANES

    @pl.when(pl.program_id(1) % steps_per_seq == 0)
    def _():
        st_ref[...] = jnp.zeros_like(st_ref)

    us = [u_ref[pl.ds(t, rows, stride=sub), :] for t in range(sub)]
    ucat = jnp.concatenate([u.astype(BF16) for u in us], axis=1)
    y_all = _dot(ucat, m1_ref[0])
    xl_ref[...] = y_all[:, width:]

    a_re = al_ref[0, 0:1, :]
    a_im = al_ref[0, 1:2, :]

    def step(c, carry):
        xr, xi = carry
        xp_ref[pl.ds(c, 1), 0:half] = xr
        xp_ref[pl.ds(c, 1), half:2 * half] = xi
        lr = xl_ref[pl.ds(c, 1), 0:half]
        li = xl_ref[pl.ds(c, 1), half:2 * half]
        return a_re * xr - a_im * xi + lr, a_re * xi + a_im * xr + li

    xr, xi = lax.fori_loop(0, rows, step, (st_ref[0:1, 0:half], st_ref[0:1, half:2 * half]))
    st_ref[0:1, 0:half] = xr
    st_ref[0:1, half:2 * half] = xi

    y_cross = _dot(xp_ref[...].astype(BF16), m2_ref[0])
    skip = d_ref[...]
    for t in range(sub):
        y = y_all[:, t * LANES:(t + 1) * LANES] + y_cross[:, t * LANES:(t + 1) * LANES] + skip * us[t]
        o_ref[pl.ds(t, rows, stride=sub), :] = _gelu_tanh(y)


def _s5_operators(a_re, a_im, log_step, b_re, b_im, c_re, c_im, sub):
    g, p = a_re.shape
    n = b_re.shape[2]
    ga = S5_LANE_GROUPS
    j = g // ga
    dt = jnp.exp(log_step.astype(F32))[:, None]
    ar, ai = a_re.astype(F32), a_im.astype(F32)
    mag = jnp.exp(ar * dt)
    abar_re = mag * jnp.cos(ai * dt)
    abar_im = mag * jnp.sin(ai * dt)
    den = ar * ar + ai * ai
    nr, ni = abar_re - 1.0, abar_im
    f_re = (nr * ar + ni * ai) / den
    f_im = (ni * ar - nr * ai) / den
    br, bi = b_re.astype(F32), b_im.astype(F32)
    bb_re = f_re[..., None] * br - f_im[..., None] * bi
    bb_im = f_re[..., None] * bi + f_im[..., None] * br
    cr, ci = c_re.astype(F32), c_im.astype(F32)
    tau = jnp.arange(sub + 1, dtype=F32)[:, None, None]
    pw_mag = jnp.exp(tau * (ar * dt)[None])
    pw_re = pw_mag * jnp.cos(tau * (ai * dt)[None])
    pw_im = pw_mag * jnp.sin(tau * (ai * dt)[None])
    hp = lax.Precision.HIGHEST
    ab_re = pw_re[..., None] * bb_re[None] - pw_im[..., None] * bb_im[None]
    ab_im = pw_re[..., None] * bb_im[None] + pw_im[..., None] * bb_re[None]
    lag = (jnp.einsum('tgpm,gnp->tgmn', ab_re[:sub], cr, precision=hp)
           - jnp.einsum('tgpm,gnp->tgmn', ab_im[:sub], ci, precision=hp))
    eye = jnp.eye(ga, dtype=F32)
    s_idx = jnp.arange(sub)
    lag_idx = s_idx[None, :] - s_idx[:, None]
    mask = (lag_idx >= 0).astype(F32)
    lag_st = lag[jnp.maximum(lag_idx, 0)] * mask[:, :, None, None, None]
    lag_st = lag_st.reshape(sub, sub, j, ga, n, n)
    toep = jnp.einsum('stjamn,ab->jsamtbn', lag_st, eye).reshape(j, sub * ga * n, sub * ga * n)
    rev = sub - 1 - s_idx
    s_re = ab_re[rev].reshape(sub, j, ga, p, n)
    s_im = ab_im[rev].reshape(sub, j, ga, p, n)
    st_re = jnp.einsum('sjapm,ab->jsambp', s_re, eye).reshape(j, sub * ga * n, ga * p)
    st_im = jnp.einsum('sjapm,ab->jsambp', s_im, eye).reshape(j, sub * ga * n, ga * p)
    m1 = jnp.concatenate([toep, st_re, st_im], axis=2).astype(BF16)
    q_re = cr[None] * pw_re[1:][:, :, None, :] - ci[None] * pw_im[1:][:, :, None, :]
    q_im = -(cr[None] * pw_im[1:][:, :, None, :] + ci[None] * pw_re[1:][:, :, None, :])
    q_re = q_re.reshape(sub, j, ga, n, p)
    q_im = q_im.reshape(sub, j, ga, n, p)
    o_re = jnp.einsum('tjanp,ab->japtbn', q_re, eye).reshape(j, ga * p, sub * ga * n)
    o_im = jnp.einsum('tjanp,ab->japtbn', q_im, eye).reshape(j, ga * p, sub * ga * n)
    m2 = jnp.concatenate([o_re, o_im], axis=1).astype(BF16)
    al = jnp.stack([pw_re[sub].reshape(j, ga * p), pw_im[sub].reshape(j, ga * p)], axis=1)
    return m1, m2, al


def s5_core(u, d_skip, m1, m2, al, batch, seq, sub):
    n_tok, w = u.shape
    j = m1.shape[0]
    half = al.shape[2]
    rows = min(S5_ROWS, seq // sub)
    step_tok = rows * sub
    steps_per_seq = seq // step_tok
    kern = functools.partial(_s5_kernel, sub=sub, rows=rows, steps_per_seq=steps_per_seq)
    return pl.pallas_call(
        kern,
        grid=(j, n_tok // step_tok),
        in_specs=[pl.BlockSpec((step_tok, LANES), lambda jj, i: (i, jj)),
                  pl.BlockSpec((1, LANES), lambda jj, i: (0, jj)),
                  pl.BlockSpec((1,) + m1.shape[1:], lambda jj, i: (jj, 0, 0)),
                  pl.BlockSpec((1,) + m2.shape[1:], lambda jj, i: (jj, 0, 0)),
                  pl.BlockSpec((1, 2, half), lambda jj, i: (jj, 0, 0))],
        out_specs=pl.BlockSpec((step_tok, LANES), lambda jj, i: (i, jj)),
        out_shape=jax.ShapeDtypeStruct((n_tok, w), F32),
        scratch_shapes=[pltpu.VMEM((rows, 2 * half), F32), pltpu.VMEM((rows, 2 * half), F32),
                        pltpu.VMEM((8, 2 * half), F32)],
        compiler_params=_params("parallel", "arbitrary"),
        name="s5_core",
    )(u, d_skip.reshape(1, w).astype(F32), m1, m2, al)


def _moba_kernel(slope_ref, q_ref, k_ref, v_ref, o_ref, km_ref, *, blk, topk, scale):
    h = pl.program_id(1)
    i = pl.program_id(2)
    nb = k_ref.shape[0] // blk
    slope = slope_ref[h]

    @pl.when(i == 0)
    def _():
        km_ref[...] = jnp.zeros_like(km_ref)

        def fill(n, c):
            kb = k_ref[pl.ds(pl.multiple_of(n * blk, blk), blk), :].astype(F32)
            km_ref[pl.ds(n, 1), :] = jnp.mean(kb, axis=0, keepdims=True)
            return c

        lax.fori_loop(0, nb, fill, 0)

    q = q_ref[...]
    gate = lax.dot_general(q.astype(F32), km_ref[...], (((1,), (1,)), ((), ())),
                           preferred_element_type=F32, precision=lax.Precision.HIGHEST)
    col = lax.broadcasted_iota(jnp.int32, gate.shape, 1)
    gate = jnp.where(col < i, gate, NEG_INF)
    chosen = []
    for _ in range(topk):
        mx = jnp.max(gate, axis=1, keepdims=True)
        idx = jnp.min(jnp.where(gate == mx, col, jnp.int32(2 ** 30)), axis=1, keepdims=True)
        chosen.append(jnp.where(mx > 0.5 * NEG_INF, idx, -1))
        gate = jnp.where(col == idx, NEG_INF, gate)

    rows = lax.broadcasted_iota(jnp.int32, (blk, blk), 0)
    cols = lax.broadcasted_iota(jnp.int32, (blk, blk), 1)
    dist = (rows - cols).astype(F32)

    start = pl.multiple_of(i * blk, blk)
    s = _dot_nt(q, k_ref[pl.ds(start, blk), :]) * scale - slope * dist
    s = jnp.where(dist >= 0, s, NEG_INF)
    m0 = jnp.max(s, axis=1, keepdims=True)
    p = jnp.exp(s - m0)
    l0 = jnp.sum(p, axis=1, keepdims=True)
    acc0 = _dot(p.astype(BF16), v_ref[pl.ds(start, blk), :])

    def body(n, carry):
        m, l, acc = carry
        st = pl.multiple_of(n * blk, blk)
        off = ((i - n) * blk).astype(F32)
        s = _dot_nt(q, k_ref[pl.ds(st, blk), :]) * scale - slope * (dist + off)
        sel = chosen[0] == n
        for c in chosen[1:]:
            sel = jnp.logical_or(sel, c == n)
        s = jnp.where(sel, s, NEG_INF)
        m_new = jnp.maximum(m, jnp.max(s, axis=1, keepdims=True))
        alpha = jnp.exp(m - m_new)
        p = jnp.exp(s - m_new)
        l = alpha * l + jnp.sum(p, axis=1, keepdims=True)
        acc = alpha * acc + _dot(p.astype(BF16), v_ref[pl.ds(st, blk), :])
        return m_new, l, acc

    m, l, acc = lax.fori_loop(0, i, body, (m0, l0, acc0))
    o_ref[...] = (acc / l).astype(o_ref.dtype)


def moba_core(qkv, batch, seq):
    d = qkv.shape[1] // 3
    heads = MOBA_HEADS
    hd = d // heads
    blk = MOBA_BLOCK
    assert seq % blk == 0
    nb = seq // blk
    topk = max(1, min(MOBA_TOPK, nb - 1))
    slopes = jnp.asarray(np.exp2(-8.0 * (np.arange(heads, dtype=np.float64) + 1.0) / heads), F32)
    kern = functools.partial(_moba_kernel, blk=blk, topk=topk, scale=hd ** -0.5)
    nb_pad = max(LANES, -(-nb // LANES) * LANES)
    return pl.pallas_call(
        kern,
        grid=(batch, heads, nb),
        in_specs=[pl.BlockSpec(memory_space=pltpu.SMEM),
                  pl.BlockSpec((blk, hd), lambda b, h, i: (b * nb + i, h)),
                  pl.BlockSpec((seq, hd), lambda b, h, i: (b, heads + h)),
                  pl.BlockSpec((seq, hd), lambda b, h, i: (b, 2 * heads + h))],
        out_specs=pl.BlockSpec((blk, hd), lambda b, h, i: (b * nb + i, h)),
        out_shape=jax.ShapeDtypeStruct((batch * seq, d), BF16),
        scratch_shapes=[pltpu.VMEM((nb_pad, hd), F32)],
        compiler_params=_params("parallel", "parallel", "arbitrary"),
        name="moba_core",
    )(slopes, qkv, qkv, qkv)


def _router_kernel(x_ref, g_ref, r_ref, xn_ref, lg_ref):
    xn = _rms(x_ref[...], g_ref[...])
    xn_ref[...] = xn
    lg_ref[...] = jnp.dot(xn, r_ref[...], preferred_element_type=F32, precision=lax.Precision.HIGHEST)


def router(x, gain, w_router, tm=512):
    m, d = x.shape
    e = w_router.shape[1]
    tm = min(tm, m)
    return pl.pallas_call(
        _router_kernel,
        grid=(m // tm,),
        in_specs=[pl.BlockSpec((tm, d), lambda i: (i, 0)),
                  pl.BlockSpec((1, d), lambda i: (0, 0)),
                  pl.BlockSpec((d, e), lambda i: (0, 0))],
        out_specs=[pl.BlockSpec((tm, d), lambda i: (i, 0)),
                   pl.BlockSpec((tm, e), lambda i: (i, 0))],
        out_shape=[jax.ShapeDtypeStruct((m, d), F32), jax.ShapeDtypeStruct((m, e), F32)],
        compiler_params=_params("parallel"),
        name="router",
    )(x, gain.reshape(1, d), w_router.astype(F32))


def _route_tables(logits, rows):
    n, e = logits.shape
    top_logits, top_idx = lax.top_k(logits, TOP_K)
    gates = jax.nn.softmax(top_logits, axis=-1).reshape(-1)
    e_flat = top_idx.reshape(-1).astype(jnp.int32)
    order = jnp.argsort(e_flat).astype(jnp.int32)
    counts = jnp.sum((e_flat[:, None] == jnp.arange(e, dtype=jnp.int32)[None, :]).astype(jnp.int32), axis=0)
    padded = ((counts + rows - 1) // rows) * rows
    start = jnp.cumsum(counts) - counts
    pend = jnp.cumsum(padded)
    pstart = pend - padded
    n_blocks = -(-(n * TOP_K) // rows) + e
    slot = jnp.arange(n_blocks * rows, dtype=jnp.int32)
    e_s = jnp.minimum(jnp.searchsorted(pend, slot, side='right'), e - 1).astype(jnp.int32)
    rank = slot - pstart[e_s]
    valid = rank < counts[e_s]
    pair = order[jnp.clip(start[e_s] + rank, 0, n * TOP_K - 1)]
    tok = pair // TOP_K
    slot_tok = jnp.where(valid, tok, 0).astype(jnp.int32)
    slot_dst = jnp.where(valid, (pair % TOP_K) * n + tok, -1).astype(jnp.int32)
    slot_gate = jnp.where(valid, gates[pair], 0.0).astype(F32)
    block_e = e_s[::rows]
    block_valid = (jnp.arange(n_blocks, dtype=jnp.int32) * rows < pend[-1]).astype(jnp.int32)
    return (block_e, block_valid, slot_tok.reshape(n_blocks, 1, rows),
            slot_dst.reshape(n_blocks, 1, rows), slot_gate.reshape(n_blocks * rows, 1))


def _moe_kernel(be_ref, bv_ref, tok_ref, dst_ref, gate_ref, xn_hbm, wg_ref, wu_ref, wd_ref, y_hbm,
                xg_ref, xb_ref, acc_ref, yb_ref, sem_in, sem_out, *, rows):
    i = pl.program_id(0)
    j = pl.program_id(1)
    valid = bv_ref[i] != 0

    def row_in(r, t):
        return pltpu.make_async_copy(xn_hbm.at[pl.ds(t, 1)], xg_ref.at[pl.ds(r, 1)], sem_in)

    def row_out(r, t):
        return pltpu.make_async_copy(yb_ref.at[pl.ds(r, 1)], y_hbm.at[pl.ds(t, 1)], sem_out)

    @pl.when(jnp.logical_and(valid, j == 0))
    def _():
        def issue(r, c):
            row_in(r, tok_ref[0, 0, r]).start()
            return c

        lax.fori_loop(0, rows, issue, 0)

        def drain(r, c):
            row_in(r, 0).wait()
            return c

        lax.fori_loop(0, rows, drain, 0)
        xb_ref[...] = xg_ref[...].astype(BF16)
        acc_ref[...] = jnp.zeros_like(acc_ref)

    @pl.when(valid)
    def _():
        acc_ref[...] += _swiglu_step(xb_ref[...], wg_ref, wu_ref, wd_ref)

    @pl.when(jnp.logical_and(valid, j == pl.num_programs(1) - 1))
    def _():
        yb_ref[...] = acc_ref[...] * gate_ref[...]

        def issue(r, c):
            t = dst_ref[0, 0, r]

            @pl.when(t >= 0)
            def _():
                row_out(r, t).start()

            return c

        lax.fori_loop(0, rows, issue, 0)

        def drain(r, c):
            t = dst_ref[0, 0, r]

            @pl.when(t >= 0)
            def _():
                row_out(r, 0).wait()

            return c

        lax.fori_loop(0, rows, drain, 0)


def moe_ffn(xn, tables, wg, wu, wd, tf=512):
    n, d = xn.shape
    f = wg.shape[2]
    block_e, block_valid, slot_tok, slot_dst, slot_gate = tables
    n_blocks, _, rows = slot_tok.shape
    tf = min(tf, f)
    kern = functools.partial(_moe_kernel, rows=rows)
    grid_spec = pltpu.PrefetchScalarGridSpec(
        num_scalar_prefetch=2,
        grid=(n_blocks, f // tf),
        in_specs=[pl.BlockSpec((1, 1, rows), lambda i, j, be, bv: (i, 0, 0), memory_space=pltpu.SMEM),
                  pl.BlockSpec((1, 1, rows), lambda i, j, be, bv: (i, 0, 0), memory_space=pltpu.SMEM),
                  pl.BlockSpec((rows, 1), lambda i, j, be, bv: (i, 0)),
                  pl.BlockSpec(memory_space=pl.ANY),
                  pl.BlockSpec((None, d, tf), lambda i, j, be, bv: (be[i], 0, j)),
                  pl.BlockSpec((None, d, tf), lambda i, j, be, bv: (be[i], 0, j)),
                  pl.BlockSpec((None, tf, d), lambda i, j, be, bv: (be[i], j, 0))],
        out_specs=pl.BlockSpec(memory_space=pl.ANY),
        scratch_shapes=[pltpu.VMEM((rows, d), F32), pltpu.VMEM((rows, d), BF16),
                        pltpu.VMEM((rows, d), F32), pltpu.VMEM((rows, d), F32),
                        pltpu.SemaphoreType.DMA(()), pltpu.SemaphoreType.DMA(())])
    return pl.pallas_call(
        kern,
        grid_spec=grid_spec,
        out_shape=jax.ShapeDtypeStruct((TOP_K * n, d), F32),
        compiler_params=_params("arbitrary", "arbitrary"),
        name="moe_ffn",
    )(block_e, block_valid, slot_tok, slot_dst, slot_gate, xn, wg, wu, wd)


def _combine_kernel(h_ref, y0_ref, y1_ref, o_ref):
    o_ref[...] = h_ref[...] + y0_ref[...] + y1_ref[...]


def _combine_norm_kernel(h_ref, y0_ref, y1_ref, g_ref, o_ref):
    o_ref[...] = _rms(h_ref[...] + y0_ref[...] + y1_ref[...], g_ref[...])


def moe_combine(h, y, final_gain=None, tm=512):
    n, d = h.shape
    tm = min(tm, n)
    y3 = y.reshape(TOP_K, n, d)
    specs = [pl.BlockSpec((tm, d), lambda i: (i, 0)),
             pl.BlockSpec((None, tm, d), lambda i: (0, i, 0)),
             pl.BlockSpec((None, tm, d), lambda i: (1, i, 0))]
    args = [h, y3, y3]
    kern = _combine_kernel
    if final_gain is not None:
        specs.append(pl.BlockSpec((1, d), lambda i: (0, 0)))
        args.append(final_gain.reshape(1, d))
        kern = _combine_norm_kernel
    return pl.pallas_call(
        kern,
        grid=(n // tm,),
        in_specs=specs,
        out_specs=pl.BlockSpec((tm, d), lambda i: (i, 0)),
        out_shape=jax.ShapeDtypeStruct((n, d), F32),
        compiler_params=_params("parallel"),
        name="moe_combine",
    )(*args)


def _final_norm_kernel(h_ref, g_ref, o_ref):
    o_ref[...] = _rms(h_ref[...], g_ref[...])


def final_norm(h, gain, tm=512):
    n, d = h.shape
    tm = min(tm, n)
    return pl.pallas_call(
        _final_norm_kernel,
        grid=(n // tm,),
        in_specs=[pl.BlockSpec((tm, d), lambda i: (i, 0)), pl.BlockSpec((1, d), lambda i: (0, 0))],
        out_specs=pl.BlockSpec((tm, d), lambda i: (i, 0)),
        out_shape=jax.ShapeDtypeStruct((n, d), F32),
        compiler_params=_params("parallel"),
        name="final_norm",
    )(h, gain.reshape(1, d))


def kernel(x, norm_mix, norm_ffn, norm_final, ret_w_in, ret_gn, ret_w_out, s5_w_in, s5_a_re, s5_a_im, s5_log_step, s5_b_re, s5_b_im, s5_c_re, s5_c_im, s5_d, s5_w_out, moba_w_in, moba_w_out, ffn_w_gate, ffn_w_up, ffn_w_down, moe_router, moe_w_gate, moe_w_up, moe_w_down):
    batch, seq, d = x.shape
    depth = norm_mix.shape[0]
    n = batch * seq
    h = x.reshape(n, d).astype(F32)
    i_ret = i_s5 = i_moba = i_dense = i_moe = 0
    out = None
    for i in range(depth):
        mixer = i % N_MIXERS
        if mixer == 0:
            proj = norm_matmul(h, norm_mix[i], ret_w_in[i_ret].astype(BF16), BF16)
            y = retention_core(proj, ret_gn[i_ret], batch, seq)
            h = matmul_residual(y, ret_w_out[i_ret].astype(BF16), h)
            i_ret += 1
        elif mixer == 1:
            sub = math.gcd(seq, S5_SUBCHUNK)
            u = norm_matmul(h, norm_mix[i], s5_w_in[i_s5].astype(BF16), F32)
            m1, m2, al = _s5_operators(s5_a_re[i_s5], s5_a_im[i_s5], s5_log_step[i_s5], s5_b_re[i_s5],
                                       s5_b_im[i_s5], s5_c_re[i_s5], s5_c_im[i_s5], sub)
            y = s5_core(u, s5_d[i_s5], m1, m2, al, batch, seq, sub)
            h = matmul_glu_residual(y, s5_w_out[i_s5].astype(BF16), h)
            i_s5 += 1
        else:
            qkv = norm_matmul(h, norm_mix[i], moba_w_in[i_moba].astype(BF16), BF16)
            o = moba_core(qkv, batch, seq)
            h = matmul_residual(o, moba_w_out[i_moba].astype(BF16), h)
            i_moba += 1
        last = i == depth - 1
        if i % 2 == 0:
            h = dense_ffn(h, norm_ffn[i], ffn_w_gate[i_dense].astype(BF16), ffn_w_up[i_dense].astype(BF16),
                          ffn_w_down[i_dense].astype(BF16))
            i_dense += 1
            if last:
                out = final_norm(h, norm_final)
        else:
            xn, logits = router(h, norm_ffn[i], moe_router[i_moe])
            tables = _route_tables(logits, min(MOE_ROWS, n))
            y = moe_ffn(xn, tables, moe_w_gate[i_moe].astype(BF16), moe_w_up[i_moe].astype(BF16),
                        moe_w_down[i_moe].astype(BF16))
            i_moe += 1
            if last:
                out = moe_combine(h, y, norm_final)
            else:
                h = moe_combine(h, y)
    return out.reshape(batch, seq, d).astype(x.dtype)
```

```python
import functools
import math

import numpy as np
import jax
import jax.numpy as jnp
from jax import lax
from jax.experimental import pallas as pl
from jax.experimental.pallas import tpu as pltpu

F32 = jnp.float32
BF16 = jnp.bfloat16

NORM_EPS = 1e-6
NEG_INF = -1e30
N_MIXERS = 3

RET_HEADS = 4
RET_CHUNK = 256
S5_GROUP = 16
S5_LANE_GROUPS = 8
S5_SUBCHUNK = 16
S5_ROWS = 128
S5_DT_MIN = 1e-3
MOBA_HEADS = 8
MOBA_BLOCK = 256
MOBA_TOPK = 3
MOBA_GROUP = 4
TOP_K = 2
MOE_ROWS = 1024
MOE_ISSUE_UNROLL = 8

V7X_VMEM_LIMIT_BYTES = 56 * 1024 * 1024
LANES = 128


def _params(*sem):
    return pltpu.CompilerParams(dimension_semantics=sem, vmem_limit_bytes=V7X_VMEM_LIMIT_BYTES)


def _rms(x, gain):
    return x * lax.rsqrt(jnp.mean(x * x, axis=-1, keepdims=True) + NORM_EPS) * gain


def _dot(a, b):
    return jnp.dot(a, b, preferred_element_type=F32)


def _dot_nt(a, b):
    return lax.dot_general(a, b, (((1,), (1,)), ((), ())), preferred_element_type=F32)


def _dot_tn(a, b):
    return lax.dot_general(a, b, (((0,), (0,)), ((), ())), preferred_element_type=F32)


def _norm_matmul_kernel(x_ref, g_ref, w_ref, o_ref, xn_ref):
    @pl.when(pl.program_id(1) == 0)
    def _():
        xn_ref[...] = _rms(x_ref[...], g_ref[...]).astype(BF16)

    o_ref[...] = _dot(xn_ref[...], w_ref[...]).astype(o_ref.dtype)


def norm_matmul(x, gain, w, out_dtype, tm=1024, tn=1024):
    m, d = x.shape
    n = w.shape[1]
    tm, tn = min(tm, m), min(tn, n)
    return pl.pallas_call(
        _norm_matmul_kernel,
        grid=(m // tm, n // tn),
        in_specs=[pl.BlockSpec((tm, d), lambda i, j: (i, 0)),
                  pl.BlockSpec((1, d), lambda i, j: (0, 0)),
                  pl.BlockSpec((d, tn), lambda i, j: (0, j))],
        out_specs=pl.BlockSpec((tm, tn), lambda i, j: (i, j)),
        out_shape=jax.ShapeDtypeStruct((m, n), out_dtype),
        scratch_shapes=[pltpu.VMEM((tm, d), BF16)],
        compiler_params=_params("parallel", "arbitrary"),
        name="norm_matmul",
    )(x, gain.reshape(1, d), w)


def _matmul_res_kernel(a_ref, w_ref, r_ref, o_ref):
    o_ref[...] = r_ref[...] + _dot(a_ref[...].astype(BF16), w_ref[...])


def matmul_residual(a, w, res, tm=512):
    m, k = a.shape
    n = w.shape[1]
    tm = min(tm, m)
    return pl.pallas_call(
        _matmul_res_kernel,
        grid=(m // tm,),
        in_specs=[pl.BlockSpec((tm, k), lambda i: (i, 0)),
                  pl.BlockSpec((k, n), lambda i: (0, 0)),
                  pl.BlockSpec((tm, n), lambda i: (i, 0))],
        out_specs=pl.BlockSpec((tm, n), lambda i: (i, 0)),
        out_shape=jax.ShapeDtypeStruct((m, n), F32),
        compiler_params=_params("parallel"),
        name="matmul_residual",
    )(a, w, res)


def _matmul_glu_res_kernel(a_ref, wa_ref, wb_ref, r_ref, o_ref):
    a = a_ref[...].astype(BF16)
    za = _dot(a, wa_ref[...])
    zb = _dot(a, wb_ref[...])
    o_ref[...] = r_ref[...] + za * jax.nn.sigmoid(zb)


def matmul_glu_residual(a, w, res, tm=512):
    m, k = a.shape
    n = w.shape[1] // 2
    tm = min(tm, m)
    return pl.pallas_call(
        _matmul_glu_res_kernel,
        grid=(m // tm,),
        in_specs=[pl.BlockSpec((tm, k), lambda i: (i, 0)),
                  pl.BlockSpec((k, n), lambda i: (0, 0)),
                  pl.BlockSpec((k, n), lambda i: (0, 1)),
                  pl.BlockSpec((tm, n), lambda i: (i, 0))],
        out_specs=pl.BlockSpec((tm, n), lambda i: (i, 0)),
        out_shape=jax.ShapeDtypeStruct((m, n), F32),
        compiler_params=_params("parallel"),
        name="matmul_glu_residual",
    )(a, w, w, res)


def _swiglu_step(xn, wg_ref, wu_ref, wd_ref):
    g = _dot(xn, wg_ref[...])
    u = _dot(xn, wu_ref[...])
    a = (g * jax.nn.sigmoid(g) * u).astype(BF16)
    return _dot(a, wd_ref[...])


def _ffn_kernel(x_ref, g_ref, wg_ref, wu_ref, wd_ref, o_ref, xn_ref, acc_ref):
    j = pl.program_id(1)

    @pl.when(j == 0)
    def _():
        xn_ref[...] = _rms(x_ref[...], g_ref[...]).astype(BF16)
        acc_ref[...] = jnp.zeros_like(acc_ref)

    acc_ref[...] += _swiglu_step(xn_ref[...], wg_ref, wu_ref, wd_ref)

    @pl.when(j == pl.num_programs(1) - 1)
    def _():
        o_ref[...] = x_ref[...] + acc_ref[...]


def dense_ffn(x, gain, wg, wu, wd, tm=1024, tf=512):
    m, d = x.shape
    f = wg.shape[1]
    tm, tf = min(tm, m), min(tf, f)
    return pl.pallas_call(
        _ffn_kernel,
        grid=(m // tm, f // tf),
        in_specs=[pl.BlockSpec((tm, d), lambda i, j: (i, 0)),
                  pl.BlockSpec((1, d), lambda i, j: (0, 0)),
                  pl.BlockSpec((d, tf), lambda i, j: (0, j)),
                  pl.BlockSpec((d, tf), lambda i, j: (0, j)),
                  pl.BlockSpec((tf, d), lambda i, j: (j, 0))],
        out_specs=pl.BlockSpec((tm, d), lambda i, j: (i, 0)),
        out_shape=jax.ShapeDtypeStruct((m, d), F32),
        scratch_shapes=[pltpu.VMEM((tm, d), BF16), pltpu.VMEM((tm, d), F32)],
        compiler_params=_params("parallel", "arbitrary"),
        name="dense_ffn",
    )(x, gain.reshape(1, d), wg, wu, wd)


def _retention_kernel(q_ref, k_ref, v_ref, g_ref, dec_ref, qd_ref, kd_ref, gn_ref, o_ref, r_ref,
                      *, heads, dk, dv, chunk_decay):
    @pl.when(pl.program_id(1) == 0)
    def _():
        r_ref[...] = jnp.zeros_like(r_ref)

    for h in range(heads):
        qh = q_ref[:, h * dk:(h + 1) * dk]
        kh = k_ref[:, h * dk:(h + 1) * dk]
        vh = v_ref[:, h * dv:(h + 1) * dv]
        s = _dot_nt(qh, kh) * dec_ref[h]
        y = _dot(s.astype(BF16), vh)
        y = y + _dot(qh, r_ref[h].astype(BF16)) * qd_ref[h]
        kd = (kh.astype(F32) * kd_ref[h]).astype(BF16)
        r_ref[h] = r_ref[h] * chunk_decay[h] + _dot_tn(kd, vh)
        mu = jnp.mean(y, axis=-1, keepdims=True)
        yc = y - mu
        var = jnp.mean(yc * yc, axis=-1, keepdims=True)
        yn = yc * lax.rsqrt(var + NORM_EPS) * gn_ref[:, h * dv:(h + 1) * dv]
        gate = g_ref[:, h * dv:(h + 1) * dv].astype(F32)
        o_ref[:, h * dv:(h + 1) * dv] = (yn * (gate * jax.nn.sigmoid(gate))).astype(o_ref.dtype)


def retention_core(proj, gn_gain, batch, seq):
    d = proj.shape[1] // 6
    heads = RET_HEADS
    dk, dv = d // heads, 2 * d // heads
    c = math.gcd(seq, RET_CHUNK)
    nc = seq // c
    log_gamma = np.log1p(-np.exp2(-5.0 - np.arange(heads, dtype=np.float64)))
    idx = np.arange(c, dtype=np.float64)
    diff = idx[:, None] - idx[None, :]
    scale = dk ** -0.5
    decay = np.where(diff >= 0, np.exp(log_gamma[:, None, None] * np.maximum(diff, 0.0)), 0.0) * scale
    q_decay = np.exp(log_gamma[:, None] * (idx + 1.0))[:, :, None]
    k_decay = np.exp(log_gamma[:, None] * (c - 1.0 - idx))[:, :, None] * scale
    chunk_decay = tuple(float(v) for v in np.exp(log_gamma * c))
    kern = functools.partial(_retention_kernel, heads=heads, dk=dk, dv=dv, chunk_decay=chunk_decay)
    row = lambda b, n: b * nc + n
    return pl.pallas_call(
        kern,
        grid=(batch, nc),
        in_specs=[pl.BlockSpec((c, d), lambda b, n: (row(b, n), 0)),
                  pl.BlockSpec((c, d), lambda b, n: (row(b, n), 1)),
                  pl.BlockSpec((c, 2 * d), lambda b, n: (row(b, n), 1)),
                  pl.BlockSpec((c, 2 * d), lambda b, n: (row(b, n), 2)),
                  pl.BlockSpec((heads, c, c), lambda b, n: (0, 0, 0)),
                  pl.BlockSpec((heads, c, 1), lambda b, n: (0, 0, 0)),
                  pl.BlockSpec((heads, c, 1), lambda b, n: (0, 0, 0)),
                  pl.BlockSpec((1, 2 * d), lambda b, n: (0, 0))],
        out_specs=pl.BlockSpec((c, 2 * d), lambda b, n: (row(b, n), 0)),
        out_shape=jax.ShapeDtypeStruct((batch * seq, 2 * d), BF16),
        scratch_shapes=[pltpu.VMEM((heads, dk, dv), F32)],
        compiler_params=_params("parallel", "arbitrary"),
        name="retention_core",
    )(proj, proj, proj, proj, jnp.asarray(decay, F32), jnp.asarray(q_decay, F32),
      jnp.asarray(k_decay, F32), gn_gain.reshape(1, 2 * d).astype(F32))


def _gelu_tanh(x):
    return 0.5 * x * (1.0 + jnp.tanh(math.sqrt(2.0 / math.pi) * (x + 0.044715 * (x * x * x))))


def _s5_expand(tc_ref, sc_ref, qc_ref, es_ref, eq_ref, m1_ref, m2_ref, *, sub, ga, half):
    width = sub * LANES
    n_ch = LANES // ga
    p = half // ga

    def keep_own_group(x, row_group, col_group):
        return jnp.where(row_group == col_group, x, 0.0).astype(BF16)

    lag = _dot(tc_ref[0], eq_ref[...])
    ra = lax.broadcasted_iota(jnp.int32, lag.shape, 0) // n_ch
    cb = (lax.broadcasted_iota(jnp.int32, lag.shape, 1) % LANES) // n_ch
    lag = keep_own_group(lag, ra, cb)
    for s in range(sub):
        rs = slice(s * LANES, (s + 1) * LANES)
        if s:
            m1_ref[rs, 0:s * LANES] = jnp.zeros((LANES, s * LANES), BF16)
        m1_ref[rs, s * LANES:width] = lag[:, 0:(sub - s) * LANES]
        sx = _dot(sc_ref[0, rs, :], es_ref[...])
        ra = lax.broadcasted_iota(jnp.int32, sx.shape, 0) // n_ch
        cb = (lax.broadcasted_iota(jnp.int32, sx.shape, 1) % half) // p
        m1_ref[rs, width:width + 2 * half] = keep_own_group(sx, ra, cb)
    for r in range(2 * ga):
        qx = _dot(qc_ref[0, r * p:(r + 1) * p, :], eq_ref[...])
        cb = (lax.broadcasted_iota(jnp.int32, qx.shape, 1) % LANES) // n_ch
        m2_ref[r * p:(r + 1) * p, :] = keep_own_group(qx, r % ga, cb)


def _s5_kernel(u_ref, d_ref, tc_ref, sc_ref, qc_ref, es_ref, eq_ref, al_ref, o_ref,
               m1_ref, m2_ref, xl_ref, xp_ref, st_ref, *, sub, rows, steps_per_seq, ga):
    half = st_ref.shape[1] // 2
    width = sub * LANES

    @pl.when(pl.program_id(1) == 0)
    def _():
        _s5_expand(tc_ref, sc_ref, qc_ref, es_ref, eq_ref, m1_ref, m2_ref, sub=sub, ga=ga, half=half)

    @pl.when(pl.program_id(1) % steps_per_seq == 0)
    def _():
        st_ref[...] = jnp.zeros_like(st_ref)

    us = [u_ref[pl.ds(t, rows, stride=sub), :] for t in range(sub)]
    ucat = jnp.concatenate([u.astype(BF16) for u in us], axis=1)
    y_all = _dot(ucat, m1_ref[...])
    xl_ref[...] = y_all[:, width:]

    a_re = al_ref[0, 0:1, :]
    a_im = al_ref[0, 1:2, :]

    def step(c, carry):
        xr, xi = carry
        xp_ref[pl.ds(c, 1), 0:half] = xr
        xp_ref[pl.ds(c, 1), half:2 * half] = xi
        lr = xl_ref[pl.ds(c, 1), 0:half]
        li = xl_ref[pl.ds(c, 1), half:2 * half]
        return a_re * xr - a_im * xi + lr, a_re * xi + a_im * xr + li

    xr, xi = lax.fori_loop(0, rows, step, (st_ref[0:1, 0:half], st_ref[0:1, half:2 * half]))
    st_ref[0:1, 0:half] = xr
    st_ref[0:1, half:2 * half] = xi

    y_cross = _dot(xp_ref[...].astype(BF16), m2_ref[...])
    skip = d_ref[...]
    for t in range(sub):
        y = y_all[:, t * LANES:(t + 1) * LANES] + y_cross[:, t * LANES:(t + 1) * LANES] + skip * us[t]
        o_ref[pl.ds(t, rows, stride=sub), :] = _gelu_tanh(y)


def _s5_operators(a_re, a_im, log_step, b_re, b_im, c_re, c_im, sub):
    g, p = a_re.shape
    n = b_re.shape[2]
    ga = S5_LANE_GROUPS
    j = g // ga
    dt = jnp.exp(log_step.astype(F32))[:, None]
    ar, ai = a_re.astype(F32), a_im.astype(F32)
    mag = jnp.exp(ar * dt)
    abar_re = mag * jnp.cos(ai * dt)
    abar_im = mag * jnp.sin(ai * dt)
    den = ar * ar + ai * ai
    nr, ni = abar_re - 1.0, abar_im
    f_re = (nr * ar + ni * ai) / den
    f_im = (ni * ar - nr * ai) / den
    br, bi = b_re.astype(F32), b_im.astype(F32)
    bb_re = f_re[..., None] * br - f_im[..., None] * bi
    bb_im = f_re[..., None] * bi + f_im[..., None] * br
    cr, ci = c_re.astype(F32), c_im.astype(F32)
    tau = jnp.arange(sub + 1, dtype=F32)[:, None, None]
    pw_mag = jnp.exp(tau * (ar * dt)[None])
    pw_re = pw_mag * jnp.cos(tau * (ai * dt)[None])
    pw_im = pw_mag * jnp.sin(tau * (ai * dt)[None])
    hp = lax.Precision.HIGHEST
    ab_re = pw_re[..., None] * bb_re[None] - pw_im[..., None] * bb_im[None]
    ab_im = pw_re[..., None] * bb_im[None] + pw_im[..., None] * bb_re[None]
    lag = (jnp.einsum('tgpm,gnp->tgmn', ab_re[:sub], cr, precision=hp)
           - jnp.einsum('tgpm,gnp->tgmn', ab_im[:sub], ci, precision=hp))
    tc = lag.reshape(sub, j, ga, n, n).transpose(1, 2, 3, 0, 4).reshape(j, ga * n, sub * n)
    rev = sub - 1 - jnp.arange(sub)
    sc = jnp.stack([ab_re[rev], ab_im[rev]], axis=0).reshape(2, sub, j, ga, p, n)
    sc = sc.transpose(2, 1, 3, 5, 0, 4).reshape(j, sub * ga * n, 2 * p)
    q_re = cr[None] * pw_re[1:][:, :, None, :] - ci[None] * pw_im[1:][:, :, None, :]
    q_im = -(cr[None] * pw_im[1:][:, :, None, :] + ci[None] * pw_re[1:][:, :, None, :])
    qc = jnp.stack([q_re, q_im], axis=0).reshape(2, sub, j, ga, n, p)
    qc = qc.transpose(2, 0, 3, 5, 1, 4).reshape(j, 2 * ga * p, sub * n)
    al = jnp.stack([pw_re[sub].reshape(j, ga * p), pw_im[sub].reshape(j, ga * p)], axis=1)
    return tc.astype(BF16), sc.astype(BF16), qc.astype(BF16), al


def _s5_spread_matrices(sub, ga, n, p):
    half = ga * p
    r = np.arange(2 * p)
    c = np.arange(2 * half)
    es = (r[:, None] // p == c[None, :] // half) & (r[:, None] % p == c[None, :] % p)
    r = np.arange(sub * n)
    c = np.arange(sub * ga * n)
    eq = (r[:, None] // n == c[None, :] // (ga * n)) & (r[:, None] % n == c[None, :] % n)
    return jnp.asarray(es, BF16), jnp.asarray(eq, BF16)


def s5_core(u, d_skip, ops, batch, seq, sub):
    tc, sc, qc, al = ops
    n_tok, w = u.shape
    j = tc.shape[0]
    half = al.shape[2]
    ga = S5_LANE_GROUPS
    es, eq = _s5_spread_matrices(sub, ga, LANES // ga, half // ga)
    rows = min(S5_ROWS, seq // sub)
    step_tok = rows * sub
    steps_per_seq = seq // step_tok
    width = sub * LANES
    kern = functools.partial(_s5_kernel, sub=sub, rows=rows, steps_per_seq=steps_per_seq, ga=ga)
    tile = lambda a: pl.BlockSpec((1,) + a.shape[1:], lambda jj, i: (jj, 0, 0))
    whole = lambda a: pl.BlockSpec(a.shape, lambda jj, i: (0, 0))
    return pl.pallas_call(
        kern,
        grid=(j, n_tok // step_tok),
        in_specs=[pl.BlockSpec((step_tok, LANES), lambda jj, i: (i, jj)),
                  pl.BlockSpec((1, LANES), lambda jj, i: (0, jj)),
                  tile(tc), tile(sc), tile(qc), whole(es), whole(eq), tile(al)],
        out_specs=pl.BlockSpec((step_tok, LANES), lambda jj, i: (i, jj)),
        out_shape=jax.ShapeDtypeStruct((n_tok, w), F32),
        scratch_shapes=[pltpu.VMEM((width, width + 2 * half), BF16), pltpu.VMEM((2 * half, width), BF16),
                        pltpu.VMEM((rows, 2 * half), F32), pltpu.VMEM((rows, 2 * half), F32),
                        pltpu.VMEM((8, 2 * half), F32)],
        compiler_params=_params("parallel", "arbitrary"),
        name="s5_core",
    )(u, d_skip.reshape(1, w).astype(F32), tc, sc, qc, es, eq, al)


MOBA_AUX_CONSTS = 6


def _moba_kernel(cst_ref, q_ref, k_ref, v_ref, o_ref, km_ref, ka_ref, vt_ref, sa_ref, sb_ref,
                 *, blk, topk, scale, group):
    h = pl.program_id(1)
    i = pl.program_id(2)
    nb = k_ref.shape[0] // blk
    hd = q_ref.shape[1]
    aux = ka_ref.shape[2] - hd
    nc = MOBA_AUX_CONSTS

    @pl.when(i == 0)
    def _():
        km_ref[...] = jnp.zeros_like(km_ref)
        jl = lax.broadcasted_iota(jnp.int32, (blk, aux), 0)
        lane = lax.broadcasted_iota(jnp.int32, (blk, aux), 1)

        def fill(n, c):
            st = pl.multiple_of(n * blk, blk)
            kb = k_ref[pl.ds(st, blk), :]
            km_ref[pl.ds(n + nc, 1), :] = jnp.mean(kb.astype(F32), axis=0, keepdims=True)
            pat = jnp.where(lane < nc // 2, n, jnp.where(lane < nc, jl, (lane - nc == n).astype(jnp.int32)))
            ka_ref[n, :, 0:hd] = kb
            ka_ref[n, :, hd:hd + aux] = pat.astype(F32).astype(BF16)
            vt_ref[n] = v_ref[pl.ds(st, blk), :].astype(F32).T.astype(BF16)
            return c

        lax.fori_loop(0, nb, fill, 0)

    q = q_ref[...]
    gate = lax.dot_general(q.astype(F32), km_ref[...], (((1,), (1,)), ((), ())),
                           preferred_element_type=F32, precision=lax.Precision.HIGHEST)
    lane = lax.broadcasted_iota(jnp.int32, gate.shape, 1)
    gate = jnp.where(jnp.logical_and(lane >= nc, lane - nc < i), gate, NEG_INF)
    chosen = jnp.zeros(gate.shape, jnp.bool_)
    lane_f = lane.astype(F32)
    for _ in range(topk):
        mx = jnp.max(gate, axis=1, keepdims=True)
        idx = jnp.min(jnp.where(gate == mx, lane_f, float(2 ** 20)), axis=1, keepdims=True)
        hit = lane_f == idx
        chosen = jnp.logical_or(chosen, jnp.logical_and(hit, mx > 0.5 * NEG_INF))
        gate = jnp.where(hit, NEG_INF, gate)

    qx = jnp.where(chosen, 0.0, NEG_INF)
    for c in range(nc):
        qx = jnp.where(lane == c, cst_ref[h, c], qx)
    qa = jnp.concatenate([q, qx.astype(BF16)], axis=1)
    a_full = cst_ref[h, 0] + cst_ref[h, 1] + cst_ref[h, 2]
    c_full = cst_ref[h, 3] + cst_ref[h, 4] + cst_ref[h, 5]

    jk = lax.broadcasted_iota(jnp.int32, (blk, blk), 0)
    jq = lax.broadcasted_iota(jnp.int32, (blk, blk), 1)
    st = _dot_nt(ka_ref[i, :, 0:hd], q) + (c_full * jk.astype(F32) + a_full * i.astype(F32))
    st = jnp.where(jk <= jq, st, NEG_INF)
    c2 = scale * math.log2(math.e)
    m0 = jnp.max(st, axis=0, keepdims=True)
    p = jnp.exp2((st - m0) * c2)
    l0 = jnp.sum(p, axis=0, keepdims=True)
    acc0 = _dot(vt_ref[i], p.astype(BF16))

    n_groups = nb // group

    half = group // 2

    def scores(g, s_ref):
        for u in range(2):
            ka = ka_ref[pl.ds(g * group + u * half, half)].reshape(half * blk, hd + aux)
            s_ref[u * half * blk:(u + 1) * half * blk, :] = _dot_nt(ka, qa)

    def update(st, g, carry):
        m, l, acc = carry
        m_new = jnp.maximum(m, jnp.max(st, axis=0, keepdims=True))
        alpha = jnp.exp2((m - m_new) * c2)
        p = jnp.exp2((st - m_new) * c2)
        l = alpha * l + jnp.sum(p, axis=0, keepdims=True)
        pb = p.astype(BF16)
        pv = _dot(vt_ref[g * group], pb[0:blk])
        for u in range(1, group):
            pv = pv + _dot(vt_ref[g * group + u], pb[u * blk:(u + 1) * blk])
        return m_new, l, alpha * acc + pv

    scores(0, sa_ref)

    def body(t, carry):
        g0 = 2 * t
        g1 = g0 + 1
        g2 = jnp.minimum(g0 + 2, n_groups - 1)
        scores(g1, sb_ref)
        carry = update(sa_ref[...], g0, carry)
        scores(g2, sa_ref)
        return update(sb_ref[...], g1, carry)

    m, l, acc = lax.fori_loop(0, (i + 2 * group - 1) // (2 * group), body, (m0, l0, acc0))
    o_ref[...] = (acc / l).T.astype(o_ref.dtype)


def moba_core(qkv, batch, seq):
    d = qkv.shape[1] // 3
    heads = MOBA_HEADS
    hd = d // heads
    blk = MOBA_BLOCK
    assert seq % blk == 0
    nb = seq // blk
    topk = max(1, min(MOBA_TOPK, nb - 1))
    group = math.gcd(nb, MOBA_GROUP)
    aux = LANES
    assert MOBA_AUX_CONSTS + nb <= aux and (nb // group) % 2 == 0
    scale = hd ** -0.5
    slopes = np.exp2(-8.0 * (np.arange(heads, dtype=np.float64) + 1.0) / heads)
    consts = []
    for val in (slopes * blk / scale, slopes / scale):
        rest = jnp.asarray(val, F32)
        for _ in range(MOBA_AUX_CONSTS // 2):
            piece = rest.astype(BF16).astype(F32)
            consts.append(piece)
            rest = rest - piece
    cst = jnp.stack(consts, axis=1)
    kern = functools.partial(_moba_kernel, blk=blk, topk=topk, scale=scale, group=group)
    return pl.pallas_call(
        kern,
        grid=(batch, heads, nb),
        in_specs=[pl.BlockSpec(memory_space=pltpu.SMEM),
                  pl.BlockSpec((blk, hd), lambda b, h, i: (b * nb + i, h)),
                  pl.BlockSpec((seq, hd), lambda b, h, i: (b, heads + h)),
                  pl.BlockSpec((seq, hd), lambda b, h, i: (b, 2 * heads + h))],
        out_specs=pl.BlockSpec((blk, hd), lambda b, h, i: (b * nb + i, h)),
        out_shape=jax.ShapeDtypeStruct((batch * seq, d), BF16),
        scratch_shapes=[pltpu.VMEM((aux, hd), F32), pltpu.VMEM((nb, blk, hd + aux), BF16),
                        pltpu.VMEM((nb, hd, blk), BF16),
                        pltpu.VMEM((group * blk, blk), F32), pltpu.VMEM((group * blk, blk), F32)],
        compiler_params=_params("parallel", "parallel", "arbitrary"),
        name="moba_core",
    )(cst, qkv, qkv, qkv)


def _router_kernel(x_ref, g_ref, r_ref, xn_ref, lg_ref):
    xn = _rms(x_ref[...], g_ref[...])
    xn_ref[...] = xn
    lg_ref[...] = jnp.dot(xn, r_ref[...], preferred_element_type=F32, precision=lax.Precision.HIGHEST)


def router(x, gain, w_router, tm=512):
    m, d = x.shape
    e = w_router.shape[1]
    tm = min(tm, m)
    return pl.pallas_call(
        _router_kernel,
        grid=(m // tm,),
        in_specs=[pl.BlockSpec((tm, d), lambda i: (i, 0)),
                  pl.BlockSpec((1, d), lambda i: (0, 0)),
                  pl.BlockSpec((d, e), lambda i: (0, 0))],
        out_specs=[pl.BlockSpec((tm, d), lambda i: (i, 0)),
                   pl.BlockSpec((tm, e), lambda i: (i, 0))],
        out_shape=[jax.ShapeDtypeStruct((m, d), F32), jax.ShapeDtypeStruct((m, e), F32)],
        compiler_params=_params("parallel"),
        name="router",
    )(x, gain.reshape(1, d), w_router.astype(F32))


def _route_tables(logits, rows):
    n, e = logits.shape
    top_logits, top_idx = lax.top_k(logits, TOP_K)
    gates = jax.nn.softmax(top_logits, axis=-1).reshape(-1)
    e_flat = top_idx.reshape(-1).astype(jnp.int32)
    order = jnp.argsort(e_flat).astype(jnp.int32)
    counts = jnp.sum((e_flat[:, None] == jnp.arange(e, dtype=jnp.int32)[None, :]).astype(jnp.int32), axis=0)
    padded = ((counts + rows - 1) // rows) * rows
    start = jnp.cumsum(counts) - counts
    pend = jnp.cumsum(padded)
    pstart = pend - padded
    n_blocks = -(-(n * TOP_K) // rows) + e
    slot = jnp.arange(n_blocks * rows, dtype=jnp.int32)
    e_s = jnp.minimum(jnp.searchsorted(pend, slot, side='right'), e - 1).astype(jnp.int32)
    rank = slot - pstart[e_s]
    valid = rank < counts[e_s]
    pair = order[jnp.clip(start[e_s] + rank, 0, n * TOP_K - 1)]
    tok = pair // TOP_K
    slot_tok = jnp.where(valid, tok, 0).astype(jnp.int32)
    slot_dst = jnp.where(valid, (pair % TOP_K) * n + tok, TOP_K * n + slot % rows).astype(jnp.int32)
    slot_gate = jnp.where(valid, gates[pair], 0.0).astype(F32)
    block_e = e_s[::rows]
    block_valid = (jnp.arange(n_blocks, dtype=jnp.int32) * rows < pend[-1]).astype(jnp.int32)
    return (block_e, block_valid, slot_tok.reshape(n_blocks, 1, rows),
            slot_dst.reshape(n_blocks, 1, rows), slot_gate.reshape(n_blocks * rows, 1))


def _moe_kernel(be_ref, bv_ref, tok_ref, dst_ref, gate_ref, xn_hbm, wg_ref, wu_ref, wd_ref, y_hbm,
                xg_ref, xb_ref, acc_ref, yb_ref, sem_in, sem_out, *, rows):
    i = pl.program_id(0)
    j = pl.program_id(1)
    valid = bv_ref[i] != 0

    def row_in(r, t):
        return pltpu.make_async_copy(xn_hbm.at[pl.ds(t, 1)], xg_ref.at[pl.ds(r, 1)], sem_in)

    def row_out(r, t):
        return pltpu.make_async_copy(yb_ref.at[pl.ds(r, 1)], y_hbm.at[pl.ds(t, 1)], sem_out)

    @pl.when(jnp.logical_and(i == 0, j == 0))
    def _():
        yb_ref[...] = jnp.zeros_like(yb_ref)
        spare = pltpu.make_async_copy(yb_ref, y_hbm.at[pl.ds(y_hbm.shape[0] - rows, rows)], sem_out)
        spare.start()
        spare.wait()

    @pl.when(jnp.logical_and(valid, j == 0))
    def _():
        def issue(r, c):
            row_in(r, tok_ref[0, 0, r]).start()
            return c

        lax.fori_loop(0, rows, issue, 0, unroll=MOE_ISSUE_UNROLL)
        pltpu.make_async_copy(xg_ref, xg_ref, sem_in).wait()
        xb_ref[...] = xg_ref[...].astype(BF16)
        acc_ref[...] = jnp.zeros_like(acc_ref)

    @pl.when(valid)
    def _():
        acc_ref[...] += _swiglu_step(xb_ref[...], wg_ref, wu_ref, wd_ref)

    @pl.when(jnp.logical_and(valid, j == pl.num_programs(1) - 1))
    def _():
        yb_ref[...] = acc_ref[...] * gate_ref[...]

        def issue(r, c):
            row_out(r, dst_ref[0, 0, r]).start()
            return c

        lax.fori_loop(0, rows, issue, 0, unroll=MOE_ISSUE_UNROLL)
        pltpu.make_async_copy(yb_ref, yb_ref, sem_out).wait()


def moe_ffn(xn, tables, wg, wu, wd, tf=512):
    n, d = xn.shape
    f = wg.shape[2]
    block_e, block_valid, slot_tok, slot_dst, slot_gate = tables
    n_blocks, _, rows = slot_tok.shape
    tf = min(tf, f)
    kern = functools.partial(_moe_kernel, rows=rows)
    grid_spec = pltpu.PrefetchScalarGridSpec(
        num_scalar_prefetch=2,
        grid=(n_blocks, f // tf),
        in_specs=[pl.BlockSpec((1, 1, rows), lambda i, j, be, bv: (i, 0, 0), memory_space=pltpu.SMEM),
                  pl.BlockSpec((1, 1, rows), lambda i, j, be, bv: (i, 0, 0), memory_space=pltpu.SMEM),
                  pl.BlockSpec((rows, 1), lambda i, j, be, bv: (i, 0)),
                  pl.BlockSpec(memory_space=pl.ANY),
                  pl.BlockSpec((None, d, tf), lambda i, j, be, bv: (be[i], 0, j)),
                  pl.BlockSpec((None, d, tf), lambda i, j, be, bv: (be[i], 0, j)),
                  pl.BlockSpec((None, tf, d), lambda i, j, be, bv: (be[i], j, 0))],
        out_specs=pl.BlockSpec(memory_space=pl.ANY),
        scratch_shapes=[pltpu.VMEM((rows, d), F32), pltpu.VMEM((rows, d), BF16),
                        pltpu.VMEM((rows, d), F32), pltpu.VMEM((rows, d), F32),
                        pltpu.SemaphoreType.DMA(()), pltpu.SemaphoreType.DMA(())])
    return pl.pallas_call(
        kern,
        grid_spec=grid_spec,
        out_shape=jax.ShapeDtypeStruct((TOP_K * n + rows, d), F32),
        compiler_params=_params("arbitrary", "arbitrary"),
        name="moe_ffn",
    )(block_e, block_valid, slot_tok, slot_dst, slot_gate, xn, wg, wu, wd)


def _combine_kernel(h_ref, y0_ref, y1_ref, o_ref):
    o_ref[...] = h_ref[...] + y0_ref[...] + y1_ref[...]


def _combine_norm_kernel(h_ref, y0_ref, y1_ref, g_ref, o_ref):
    o_ref[...] = _rms(h_ref[...] + y0_ref[...] + y1_ref[...], g_ref[...])


def moe_combine(h, y, final_gain=None, tm=512):
    n, d = h.shape
    tm = min(tm, n)
    nt = n // tm
    specs = [pl.BlockSpec((tm, d), lambda i: (i, 0)),
             pl.BlockSpec((tm, d), lambda i: (i, 0)),
             pl.BlockSpec((tm, d), lambda i: (nt + i, 0))]
    args = [h, y, y]
    kern = _combine_kernel
    if final_gain is not None:
        specs.append(pl.BlockSpec((1, d), lambda i: (0, 0)))
        args.append(final_gain.reshape(1, d))
        kern = _combine_norm_kernel
    return pl.pallas_call(
        kern,
        grid=(n // tm,),
        in_specs=specs,
        out_specs=pl.BlockSpec((tm, d), lambda i: (i, 0)),
        out_shape=jax.ShapeDtypeStruct((n, d), F32),
        compiler_params=_params("parallel"),
        name="moe_combine",
    )(*args)


def _final_norm_kernel(h_ref, g_ref, o_ref):
    o_ref[...] = _rms(h_ref[...], g_ref[...])


def final_norm(h, gain, tm=512):
    n, d = h.shape
    tm = min(tm, n)
    return pl.pallas_call(
        _final_norm_kernel,
        grid=(n // tm,),
        in_specs=[pl.BlockSpec((tm, d), lambda i: (i, 0)), pl.BlockSpec((1, d), lambda i: (0, 0))],
        out_specs=pl.BlockSpec((tm, d), lambda i: (i, 0)),
        out_shape=jax.ShapeDtypeStruct((n, d), F32),
        compiler_params=_params("parallel"),
        name="final_norm",
    )(h, gain.reshape(1, d))


def kernel(x, norm_mix, norm_ffn, norm_final, ret_w_in, ret_gn, ret_w_out, s5_w_in, s5_a_re, s5_a_im, s5_log_step, s5_b_re, s5_b_im, s5_c_re, s5_c_im, s5_d, s5_w_out, moba_w_in, moba_w_out, ffn_w_gate, ffn_w_up, ffn_w_down, moe_router, moe_w_gate, moe_w_up, moe_w_down):
    batch, seq, d = x.shape
    depth = norm_mix.shape[0]
    n = batch * seq
    h = x.reshape(n, d).astype(F32)
    i_ret = i_s5 = i_moba = i_dense = i_moe = 0
    out = None
    for i in range(depth):
        mixer = i % N_MIXERS
        if mixer == 0:
            proj = norm_matmul(h, norm_mix[i], ret_w_in[i_ret].astype(BF16), BF16)
            y = retention_core(proj, ret_gn[i_ret], batch, seq)
            h = matmul_residual(y, ret_w_out[i_ret].astype(BF16), h)
            i_ret += 1
        elif mixer == 1:
            sub = math.gcd(seq, S5_SUBCHUNK)
            u = norm_matmul(h, norm_mix[i], s5_w_in[i_s5].astype(BF16), F32)
            ops = _s5_operators(s5_a_re[i_s5], s5_a_im[i_s5], s5_log_step[i_s5], s5_b_re[i_s5],
                                s5_b_im[i_s5], s5_c_re[i_s5], s5_c_im[i_s5], sub)
            y = s5_core(u, s5_d[i_s5], ops, batch, seq, sub)
            h = matmul_glu_residual(y, s5_w_out[i_s5].astype(BF16), h)
            i_s5 += 1
        else:
            qkv = norm_matmul(h, norm_mix[i], moba_w_in[i_moba].astype(BF16), BF16)
            o = moba_core(qkv, batch, seq)
            h = matmul_residual(o, moba_w_out[i_moba].astype(BF16), h)
            i_moba += 1
        last = i == depth - 1
        if i % 2 == 0:
            h = dense_ffn(h, norm_ffn[i], ffn_w_gate[i_dense].astype(BF16), ffn_w_up[i_dense].astype(BF16),
                          ffn_w_down[i_dense].astype(BF16))
            i_dense += 1
            if last:
                out = final_norm(h, norm_final)
        else:
            xn, logits = router(h, norm_ffn[i], moe_router[i_moe])
            tables = _route_tables(logits, min(MOE_ROWS, n))
            y = moe_ffn(xn, tables, moe_w_gate[i_moe].astype(BF16), moe_w_up[i_moe].astype(BF16),
                        moe_w_down[i_moe].astype(BF16))
            i_moe += 1
            if last:
                out = moe_combine(h, y, norm_final)
            else:
                h = moe_combine(h, y)
    return out.reshape(batch, seq, d).astype(x.dtype)
```

```python
import functools
import math

import numpy as np
import jax
import jax.numpy as jnp
from jax import lax
from jax.experimental import pallas as pl
from jax.experimental.pallas import tpu as pltpu

F32 = jnp.float32
BF16 = jnp.bfloat16

NORM_EPS = 1e-6
NEG_INF = -1e30
N_MIXERS = 3

RET_HEADS = 4
RET_CHUNK = 256
S5_GROUP = 16
S5_LANE_GROUPS = 8
S5_SUBCHUNK = 16
S5_ROWS = 128
S5_DT_MIN = 1e-3
MOBA_HEADS = 8
MOBA_BLOCK = 256
MOBA_TOPK = 3
MOBA_GROUP = 4
TOP_K = 2
MOE_ROWS = 1024
MOE_ISSUE_UNROLL = 8

V7X_VMEM_LIMIT_BYTES = 56 * 1024 * 1024
LANES = 128
SUBLANES = 8


def _params(*sem):
    return pltpu.CompilerParams(dimension_semantics=sem, vmem_limit_bytes=V7X_VMEM_LIMIT_BYTES)


def _rms(x, gain):
    return x * lax.rsqrt(jnp.mean(x * x, axis=-1, keepdims=True) + NORM_EPS) * gain


def _dot(a, b):
    return jnp.dot(a, b, preferred_element_type=F32)


def _dot_nt(a, b):
    return lax.dot_general(a, b, (((1,), (1,)), ((), ())), preferred_element_type=F32)


def _dot_tn(a, b):
    return lax.dot_general(a, b, (((0,), (0,)), ((), ())), preferred_element_type=F32)


def _norm_matmul_kernel(x_ref, g_ref, w_ref, o_ref, xn_ref):
    @pl.when(pl.program_id(1) == 0)
    def _():
        xn_ref[...] = _rms(x_ref[...], g_ref[...]).astype(BF16)

    o_ref[...] = _dot(xn_ref[...], w_ref[...].astype(BF16)).astype(o_ref.dtype)


def norm_matmul(x, gain, w, layer, out_dtype, tm=1024, tn=1024):
    m, d = x.shape
    n = w.shape[2]
    tm, tn = min(tm, m), min(tn, n)
    return pl.pallas_call(
        _norm_matmul_kernel,
        grid=(m // tm, n // tn),
        in_specs=[pl.BlockSpec((tm, d), lambda i, j: (i, 0)),
                  pl.BlockSpec((1, d), lambda i, j: (0, 0)),
                  pl.BlockSpec((None, d, tn), lambda i, j: (layer, 0, j))],
        out_specs=pl.BlockSpec((tm, tn), lambda i, j: (i, j)),
        out_shape=jax.ShapeDtypeStruct((m, n), out_dtype),
        scratch_shapes=[pltpu.VMEM((tm, d), BF16)],
        compiler_params=_params("parallel", "arbitrary"),
        name="norm_matmul",
    )(x, gain.reshape(1, d), w)


def _matmul_res_kernel(a_ref, w_ref, r_ref, o_ref, wb_ref):
    @pl.when(pl.program_id(0) == 0)
    def _():
        wb_ref[...] = w_ref[...].astype(BF16)

    o_ref[...] = r_ref[...] + _dot(a_ref[...].astype(BF16), wb_ref[...])


def matmul_residual(a, w, layer, res, tm=512):
    m, k = a.shape
    n = w.shape[2]
    tm = min(tm, m)
    return pl.pallas_call(
        _matmul_res_kernel,
        grid=(m // tm,),
        in_specs=[pl.BlockSpec((tm, k), lambda i: (i, 0)),
                  pl.BlockSpec((None, k, n), lambda i: (layer, 0, 0)),
                  pl.BlockSpec((tm, n), lambda i: (i, 0))],
        out_specs=pl.BlockSpec((tm, n), lambda i: (i, 0)),
        out_shape=jax.ShapeDtypeStruct((m, n), F32),
        scratch_shapes=[pltpu.VMEM((k, n), BF16)],
        compiler_params=_params("arbitrary"),
        name="matmul_residual",
    )(a, w, res)


def _matmul_glu_res_kernel(a_ref, wa_ref, wb_ref, r_ref, o_ref, wa_bf_ref, wb_bf_ref):
    @pl.when(pl.program_id(0) == 0)
    def _():
        wa_bf_ref[...] = wa_ref[...].astype(BF16)
        wb_bf_ref[...] = wb_ref[...].astype(BF16)

    a = a_ref[...].astype(BF16)
    za = _dot(a, wa_bf_ref[...])
    zb = _dot(a, wb_bf_ref[...])
    o_ref[...] = r_ref[...] + za * jax.nn.sigmoid(zb)


def matmul_glu_residual(a, w, layer, res, tm=512):
    m, k = a.shape
    n = w.shape[2] // 2
    tm = min(tm, m)
    return pl.pallas_call(
        _matmul_glu_res_kernel,
        grid=(m // tm,),
        in_specs=[pl.BlockSpec((tm, k), lambda i: (i, 0)),
                  pl.BlockSpec((None, k, n), lambda i: (layer, 0, 0)),
                  pl.BlockSpec((None, k, n), lambda i: (layer, 0, 1)),
                  pl.BlockSpec((tm, n), lambda i: (i, 0))],
        out_specs=pl.BlockSpec((tm, n), lambda i: (i, 0)),
        out_shape=jax.ShapeDtypeStruct((m, n), F32),
        scratch_shapes=[pltpu.VMEM((k, n), BF16), pltpu.VMEM((k, n), BF16)],
        compiler_params=_params("arbitrary"),
        name="matmul_glu_residual",
    )(a, w, w, res)


def _swiglu_step(xn, wg_ref, wu_ref, wd_ref):
    g = _dot(xn, wg_ref[...].astype(BF16))
    u = _dot(xn, wu_ref[...].astype(BF16))
    a = (g * jax.nn.sigmoid(g) * u).astype(BF16)
    return _dot(a, wd_ref[...].astype(BF16))


def _ffn_kernel(x_ref, g_ref, wg_ref, wu_ref, wd_ref, o_ref, xn_ref, acc_ref):
    j = pl.program_id(1)

    @pl.when(j == 0)
    def _():
        xn_ref[...] = _rms(x_ref[...], g_ref[...]).astype(BF16)
        acc_ref[...] = jnp.zeros_like(acc_ref)

    acc_ref[...] += _swiglu_step(xn_ref[...], wg_ref, wu_ref, wd_ref)

    @pl.when(j == pl.num_programs(1) - 1)
    def _():
        o_ref[...] = x_ref[...] + acc_ref[...]


def dense_ffn(x, gain, wg, wu, wd, layer, tm=1024, tf=512):
    m, d = x.shape
    f = wg.shape[2]
    tm, tf = min(tm, m), min(tf, f)
    return pl.pallas_call(
        _ffn_kernel,
        grid=(m // tm, f // tf),
        in_specs=[pl.BlockSpec((tm, d), lambda i, j: (i, 0)),
                  pl.BlockSpec((1, d), lambda i, j: (0, 0)),
                  pl.BlockSpec((None, d, tf), lambda i, j: (layer, 0, j)),
                  pl.BlockSpec((None, d, tf), lambda i, j: (layer, 0, j)),
                  pl.BlockSpec((None, tf, d), lambda i, j: (layer, j, 0))],
        out_specs=pl.BlockSpec((tm, d), lambda i, j: (i, 0)),
        out_shape=jax.ShapeDtypeStruct((m, d), F32),
        scratch_shapes=[pltpu.VMEM((tm, d), BF16), pltpu.VMEM((tm, d), F32)],
        compiler_params=_params("parallel", "arbitrary"),
        name="dense_ffn",
    )(x, gain.reshape(1, d), wg, wu, wd)


def _retention_kernel(q_ref, k_ref, v_ref, g_ref, dec_ref, qd_ref, kd_ref, gn_ref, o_ref, r_ref,
                      *, heads, dk, dv, chunk_decay):
    @pl.when(pl.program_id(1) == 0)
    def _():
        r_ref[...] = jnp.zeros_like(r_ref)

    for h in range(heads):
        qh = q_ref[:, h * dk:(h + 1) * dk]
        kh = k_ref[:, h * dk:(h + 1) * dk]
        vh = v_ref[:, h * dv:(h + 1) * dv]
        s = _dot_nt(qh, kh) * dec_ref[h]
        y = _dot(s.astype(BF16), vh)
        y = y + _dot(qh, r_ref[h].astype(BF16)) * qd_ref[h]
        kd = (kh.astype(F32) * kd_ref[h]).astype(BF16)
        r_ref[h] = r_ref[h] * chunk_decay[h] + _dot_tn(kd, vh)
        mu = jnp.mean(y, axis=-1, keepdims=True)
        yc = y - mu
        var = jnp.mean(yc * yc, axis=-1, keepdims=True)
        yn = yc * lax.rsqrt(var + NORM_EPS) * gn_ref[:, h * dv:(h + 1) * dv]
        gate = g_ref[:, h * dv:(h + 1) * dv].astype(F32)
        o_ref[:, h * dv:(h + 1) * dv] = (yn * (gate * jax.nn.sigmoid(gate))).astype(o_ref.dtype)


def retention_core(proj, gn_gain, batch, seq):
    d = proj.shape[1] // 6
    heads = RET_HEADS
    dk, dv = d // heads, 2 * d // heads
    c = math.gcd(seq, RET_CHUNK)
    nc = seq // c
    log_gamma = np.log1p(-np.exp2(-5.0 - np.arange(heads, dtype=np.float64)))
    idx = np.arange(c, dtype=np.float64)
    diff = idx[:, None] - idx[None, :]
    scale = dk ** -0.5
    decay = np.where(diff >= 0, np.exp(log_gamma[:, None, None] * np.maximum(diff, 0.0)), 0.0) * scale
    q_decay = np.exp(log_gamma[:, None] * (idx + 1.0))[:, :, None]
    k_decay = np.exp(log_gamma[:, None] * (c - 1.0 - idx))[:, :, None] * scale
    chunk_decay = tuple(float(v) for v in np.exp(log_gamma * c))
    kern = functools.partial(_retention_kernel, heads=heads, dk=dk, dv=dv, chunk_decay=chunk_decay)
    row = lambda b, n: b * nc + n
    return pl.pallas_call(
        kern,
        grid=(batch, nc),
        in_specs=[pl.BlockSpec((c, d), lambda b, n: (row(b, n), 0)),
                  pl.BlockSpec((c, d), lambda b, n: (row(b, n), 1)),
                  pl.BlockSpec((c, 2 * d), lambda b, n: (row(b, n), 1)),
                  pl.BlockSpec((c, 2 * d), lambda b, n: (row(b, n), 2)),
                  pl.BlockSpec((heads, c, c), lambda b, n: (0, 0, 0)),
                  pl.BlockSpec((heads, c, 1), lambda b, n: (0, 0, 0)),
                  pl.BlockSpec((heads, c, 1), lambda b, n: (0, 0, 0)),
                  pl.BlockSpec((1, 2 * d), lambda b, n: (0, 0))],
        out_specs=pl.BlockSpec((c, 2 * d), lambda b, n: (row(b, n), 0)),
        out_shape=jax.ShapeDtypeStruct((batch * seq, 2 * d), BF16),
        scratch_shapes=[pltpu.VMEM((heads, dk, dv), F32)],
        compiler_params=_params("parallel", "arbitrary"),
        name="retention_core",
    )(proj, proj, proj, proj, jnp.asarray(decay, F32), jnp.asarray(q_decay, F32),
      jnp.asarray(k_decay, F32), gn_gain.reshape(1, 2 * d).astype(F32))


def _gelu_tanh(x):
    return 0.5 * x * (1.0 + jnp.tanh(math.sqrt(2.0 / math.pi) * (x + 0.044715 * (x * x * x))))


def _s5_expand(tc_ref, sc_ref, qc_ref, es_ref, eq_ref, m1_ref, m2_ref, *, sub, ga, half):
    width = sub * LANES
    n_ch = LANES // ga
    p = half // ga

    def keep_own_group(x, row_group, col_group):
        return jnp.where(row_group == col_group, x, 0.0).astype(BF16)

    lag = _dot(tc_ref[0], eq_ref[...])
    ra = lax.broadcasted_iota(jnp.int32, lag.shape, 0) // n_ch
    cb = (lax.broadcasted_iota(jnp.int32, lag.shape, 1) % LANES) // n_ch
    lag = keep_own_group(lag, ra, cb)
    for s in range(sub):
        rs = slice(s * LANES, (s + 1) * LANES)
        if s:
            m1_ref[rs, 0:s * LANES] = jnp.zeros((LANES, s * LANES), BF16)
        m1_ref[rs, s * LANES:width] = lag[:, 0:(sub - s) * LANES]
        sx = _dot(sc_ref[0, rs, :], es_ref[...])
        ra = lax.broadcasted_iota(jnp.int32, sx.shape, 0) // n_ch
        cb = (lax.broadcasted_iota(jnp.int32, sx.shape, 1) % half) // p
        m1_ref[rs, width:width + 2 * half] = keep_own_group(sx, ra, cb)
    for r in range(2 * ga):
        qx = _dot(qc_ref[0, r * p:(r + 1) * p, :], eq_ref[...])
        cb = (lax.broadcasted_iota(jnp.int32, qx.shape, 1) % LANES) // n_ch
        m2_ref[r * p:(r + 1) * p, :] = keep_own_group(qx, r % ga, cb)


def _s5_kernel(u_ref, d_ref, tc_ref, sc_ref, qc_ref, es_ref, eq_ref, al_ref, o_ref,
               m1_ref, m2_ref, xl_ref, xp_ref, st_ref, *, sub, rows, steps_per_seq, ga):
    half = st_ref.shape[1] // 2
    width = sub * LANES

    @pl.when(pl.program_id(1) == 0)
    def _():
        _s5_expand(tc_ref, sc_ref, qc_ref, es_ref, eq_ref, m1_ref, m2_ref, sub=sub, ga=ga, half=half)

    @pl.when(pl.program_id(1) % steps_per_seq == 0)
    def _():
        st_ref[...] = jnp.zeros_like(st_ref)

    us = [u_ref[pl.ds(t, rows, stride=sub), :] for t in range(sub)]
    ucat = jnp.concatenate([u.astype(BF16) for u in us], axis=1)
    y_all = _dot(ucat, m1_ref[...])
    xl_ref[...] = y_all[:, width:]

    a_re = al_ref[0, 0:1, :]
    a_im = al_ref[0, 1:2, :]

    def step(c, carry):
        xr, xi = carry
        xp_ref[pl.ds(c, 1), 0:half] = xr
        xp_ref[pl.ds(c, 1), half:2 * half] = xi
        lr = xl_ref[pl.ds(c, 1), 0:half]
        li = xl_ref[pl.ds(c, 1), half:2 * half]
        return a_re * xr - a_im * xi + lr, a_re * xi + a_im * xr + li

    xr, xi = lax.fori_loop(0, rows, step, (st_ref[0:1, 0:half], st_ref[0:1, half:2 * half]))
    st_ref[0:1, 0:half] = xr
    st_ref[0:1, half:2 * half] = xi

    y_cross = _dot(xp_ref[...].astype(BF16), m2_ref[...])
    skip = d_ref[...]
    for t in range(sub):
        y = y_all[:, t * LANES:(t + 1) * LANES] + y_cross[:, t * LANES:(t + 1) * LANES] + skip * us[t]
        o_ref[pl.ds(t, rows, stride=sub), :] = _gelu_tanh(y)


def _s5_operators(a_re, a_im, log_step, b_re, b_im, c_re, c_im, sub):
    g, p = a_re.shape
    n = b_re.shape[2]
    ga = S5_LANE_GROUPS
    j = g // ga
    dt = jnp.exp(log_step.astype(F32))[:, None]
    ar, ai = a_re.astype(F32), a_im.astype(F32)
    mag = jnp.exp(ar * dt)
    abar_re = mag * jnp.cos(ai * dt)
    abar_im = mag * jnp.sin(ai * dt)
    den = ar * ar + ai * ai
    nr, ni = abar_re - 1.0, abar_im
    f_re = (nr * ar + ni * ai) / den
    f_im = (ni * ar - nr * ai) / den
    br, bi = b_re.astype(F32), b_im.astype(F32)
    bb_re = f_re[..., None] * br - f_im[..., None] * bi
    bb_im = f_re[..., None] * bi + f_im[..., None] * br
    cr, ci = c_re.astype(F32), c_im.astype(F32)
    tau = jnp.arange(sub + 1, dtype=F32)[:, None, None]
    pw_mag = jnp.exp(tau * (ar * dt)[None])
    pw_re = pw_mag * jnp.cos(tau * (ai * dt)[None])
    pw_im = pw_mag * jnp.sin(tau * (ai * dt)[None])
    hp = lax.Precision.HIGHEST
    ab_re = pw_re[..., None] * bb_re[None] - pw_im[..., None] * bb_im[None]
    ab_im = pw_re[..., None] * bb_im[None] + pw_im[..., None] * bb_re[None]
    lag = (jnp.einsum('tgpm,gnp->tgmn', ab_re[:sub], cr, precision=hp)
           - jnp.einsum('tgpm,gnp->tgmn', ab_im[:sub], ci, precision=hp))
    tc = lag.reshape(sub, j, ga, n, n).transpose(1, 2, 3, 0, 4).reshape(j, ga * n, sub * n)
    rev = sub - 1 - jnp.arange(sub)
    sc = jnp.stack([ab_re[rev], ab_im[rev]], axis=0).reshape(2, sub, j, ga, p, n)
    sc = sc.transpose(2, 1, 3, 5, 0, 4).reshape(j, sub * ga * n, 2 * p)
    q_re = cr[None] * pw_re[1:][:, :, None, :] - ci[None] * pw_im[1:][:, :, None, :]
    q_im = -(cr[None] * pw_im[1:][:, :, None, :] + ci[None] * pw_re[1:][:, :, None, :])
    qc = jnp.stack([q_re, q_im], axis=0).reshape(2, sub, j, ga, n, p)
    qc = qc.transpose(2, 0, 3, 5, 1, 4).reshape(j, 2 * ga * p, sub * n)
    al = jnp.stack([pw_re[sub].reshape(j, ga * p), pw_im[sub].reshape(j, ga * p)], axis=1)
    return tc.astype(BF16), sc.astype(BF16), qc.astype(BF16), al


def _s5_spread_matrices(sub, ga, n, p):
    half = ga * p
    r = np.arange(2 * p)
    c = np.arange(2 * half)
    es = (r[:, None] // p == c[None, :] // half) & (r[:, None] % p == c[None, :] % p)
    r = np.arange(sub * n)
    c = np.arange(sub * ga * n)
    eq = (r[:, None] // n == c[None, :] // (ga * n)) & (r[:, None] % n == c[None, :] % n)
    return jnp.asarray(es, BF16), jnp.asarray(eq, BF16)


def s5_core(u, d_skip, ops, batch, seq, sub):
    tc, sc, qc, al = ops
    n_tok, w = u.shape
    j = tc.shape[0]
    half = al.shape[2]
    ga = S5_LANE_GROUPS
    es, eq = _s5_spread_matrices(sub, ga, LANES // ga, half // ga)
    rows = min(S5_ROWS, seq // sub)
    step_tok = rows * sub
    steps_per_seq = seq // step_tok
    width = sub * LANES
    kern = functools.partial(_s5_kernel, sub=sub, rows=rows, steps_per_seq=steps_per_seq, ga=ga)
    tile = lambda a: pl.BlockSpec((1,) + a.shape[1:], lambda jj, i: (jj, 0, 0))
    whole = lambda a: pl.BlockSpec(a.shape, lambda jj, i: (0, 0))
    return pl.pallas_call(
        kern,
        grid=(j, n_tok // step_tok),
        in_specs=[pl.BlockSpec((step_tok, LANES), lambda jj, i: (i, jj)),
                  pl.BlockSpec((1, LANES), lambda jj, i: (0, jj)),
                  tile(tc), tile(sc), tile(qc), whole(es), whole(eq), tile(al)],
        out_specs=pl.BlockSpec((step_tok, LANES), lambda jj, i: (i, jj)),
        out_shape=jax.ShapeDtypeStruct((n_tok, w), F32),
        scratch_shapes=[pltpu.VMEM((width, width + 2 * half), BF16), pltpu.VMEM((2 * half, width), BF16),
                        pltpu.VMEM((rows, 2 * half), F32), pltpu.VMEM((rows, 2 * half), F32),
                        pltpu.VMEM((8, 2 * half), F32)],
        compiler_params=_params("parallel", "arbitrary"),
        name="s5_core",
    )(u, d_skip.reshape(1, w).astype(F32), tc, sc, qc, es, eq, al)


MOBA_AUX_CONSTS = 6


def _moba_kernel(cst_ref, q_ref, k_ref, v_ref, o_ref, km_ref, ka_ref, vt_ref, sa_ref, sb_ref,
                 *, blk, topk, scale, group):
    h = pl.program_id(1)
    i = pl.program_id(2)
    nb = k_ref.shape[0] // blk
    hd = q_ref.shape[1]
    aux = ka_ref.shape[2] - hd
    nc = MOBA_AUX_CONSTS

    @pl.when(i == 0)
    def _():
        km_ref[...] = jnp.zeros_like(km_ref)
        jl = lax.broadcasted_iota(jnp.int32, (blk, aux), 0)
        lane = lax.broadcasted_iota(jnp.int32, (blk, aux), 1)

        def fill(n, c):
            st = pl.multiple_of(n * blk, blk)
            kb = k_ref[pl.ds(st, blk), :]
            km_ref[pl.ds(n + nc, 1), :] = jnp.mean(kb.astype(F32), axis=0, keepdims=True)
            pat = jnp.where(lane < nc // 2, n, jnp.where(lane < nc, jl, (lane - nc == n).astype(jnp.int32)))
            ka_ref[n, :, 0:hd] = kb
            ka_ref[n, :, hd:hd + aux] = pat.astype(F32).astype(BF16)
            vt_ref[n] = v_ref[pl.ds(st, blk), :].astype(F32).T.astype(BF16)
            return c

        lax.fori_loop(0, nb, fill, 0)

    q = q_ref[...]
    gate = lax.dot_general(q.astype(F32), km_ref[...], (((1,), (1,)), ((), ())),
                           preferred_element_type=F32, precision=lax.Precision.HIGHEST)
    lane = lax.broadcasted_iota(jnp.int32, gate.shape, 1)
    gate = jnp.where(jnp.logical_and(lane >= nc, lane - nc < i), gate, NEG_INF)
    chosen = jnp.zeros(gate.shape, jnp.bool_)
    lane_f = lane.astype(F32)
    for _ in range(topk):
        mx = jnp.max(gate, axis=1, keepdims=True)
        idx = jnp.min(jnp.where(gate == mx, lane_f, float(2 ** 20)), axis=1, keepdims=True)
        hit = lane_f == idx
        chosen = jnp.logical_or(chosen, jnp.logical_and(hit, mx > 0.5 * NEG_INF))
        gate = jnp.where(hit, NEG_INF, gate)

    qx = jnp.where(chosen, 0.0, NEG_INF)
    for c in range(nc):
        qx = jnp.where(lane == c, cst_ref[h, c], qx)
    qa = jnp.concatenate([q, qx.astype(BF16)], axis=1)
    a_full = cst_ref[h, 0] + cst_ref[h, 1] + cst_ref[h, 2]
    c_full = cst_ref[h, 3] + cst_ref[h, 4] + cst_ref[h, 5]

    jk = lax.broadcasted_iota(jnp.int32, (blk, blk), 0)
    jq = lax.broadcasted_iota(jnp.int32, (blk, blk), 1)
    st = _dot_nt(ka_ref[i, :, 0:hd], q) + (c_full * jk.astype(F32) + a_full * i.astype(F32))
    st = jnp.where(jk <= jq, st, NEG_INF)
    c2 = scale * math.log2(math.e)
    m0 = jnp.max(st, axis=0, keepdims=True)
    p = jnp.exp2((st - m0) * c2)
    l0 = jnp.sum(p, axis=0, keepdims=True)
    acc0 = _dot(vt_ref[i], p.astype(BF16))

    n_groups = nb // group

    half = group // 2

    def scores(g, s_ref):
        for u in range(2):
            ka = ka_ref[pl.ds(g * group + u * half, half)].reshape(half * blk, hd + aux)
            s_ref[u * half * blk:(u + 1) * half * blk, :] = _dot_nt(ka, qa)

    def update(st, g, carry):
        m, l, acc = carry
        m_new = jnp.maximum(m, jnp.max(st, axis=0, keepdims=True))
        alpha = jnp.exp2((m - m_new) * c2)
        p = jnp.exp2((st - m_new) * c2)
        l = alpha * l + jnp.sum(p, axis=0, keepdims=True)
        pb = p.astype(BF16)
        pv = _dot(vt_ref[g * group], pb[0:blk])
        for u in range(1, group):
            pv = pv + _dot(vt_ref[g * group + u], pb[u * blk:(u + 1) * blk])
        return m_new, l, alpha * acc + pv

    scores(0, sa_ref)

    def body(t, carry):
        g0 = 2 * t
        g1 = g0 + 1
        g2 = jnp.minimum(g0 + 2, n_groups - 1)
        scores(g1, sb_ref)
        carry = update(sa_ref[...], g0, carry)
        scores(g2, sa_ref)
        return update(sb_ref[...], g1, carry)

    m, l, acc = lax.fori_loop(0, (i + 2 * group - 1) // (2 * group), body, (m0, l0, acc0))
    o_ref[...] = (acc / l).T.astype(o_ref.dtype)


def moba_core(qkv, batch, seq):
    d = qkv.shape[1] // 3
    heads = MOBA_HEADS
    hd = d // heads
    blk = MOBA_BLOCK
    assert seq % blk == 0
    nb = seq // blk
    topk = max(1, min(MOBA_TOPK, nb - 1))
    group = math.gcd(nb, MOBA_GROUP)
    aux = LANES
    assert MOBA_AUX_CONSTS + nb <= aux and (nb // group) % 2 == 0
    scale = hd ** -0.5
    slopes = np.exp2(-8.0 * (np.arange(heads, dtype=np.float64) + 1.0) / heads)
    consts = []
    for val in (slopes * blk / scale, slopes / scale):
        rest = jnp.asarray(val, F32)
        for _ in range(MOBA_AUX_CONSTS // 2):
            piece = rest.astype(BF16).astype(F32)
            consts.append(piece)
            rest = rest - piece
    cst = jnp.stack(consts, axis=1)
    kern = functools.partial(_moba_kernel, blk=blk, topk=topk, scale=scale, group=group)
    return pl.pallas_call(
        kern,
        grid=(batch, heads, nb),
        in_specs=[pl.BlockSpec(memory_space=pltpu.SMEM),
                  pl.BlockSpec((blk, hd), lambda b, h, i: (b * nb + i, h)),
                  pl.BlockSpec((seq, hd), lambda b, h, i: (b, heads + h)),
                  pl.BlockSpec((seq, hd), lambda b, h, i: (b, 2 * heads + h))],
        out_specs=pl.BlockSpec((blk, hd), lambda b, h, i: (b * nb + i, h)),
        out_shape=jax.ShapeDtypeStruct((batch * seq, d), BF16),
        scratch_shapes=[pltpu.VMEM((aux, hd), F32), pltpu.VMEM((nb, blk, hd + aux), BF16),
                        pltpu.VMEM((nb, hd, blk), BF16),
                        pltpu.VMEM((group * blk, blk), F32), pltpu.VMEM((group * blk, blk), F32)],
        compiler_params=_params("parallel", "parallel", "arbitrary"),
        name="moba_core",
    )(cst, qkv, qkv, qkv)


def _to_token_tiles(ref, x):
    pieces = x.shape[1] // LANES
    for s in range(pieces):
        ref[pl.ds(s, x.shape[0], stride=pieces), :] = x[:, s * LANES:(s + 1) * LANES]


def _from_token_tiles(ref, tokens, pieces):
    return [ref[pl.ds(s, tokens, stride=pieces), :] for s in range(pieces)]


def _router_kernel(x_ref, g_ref, r_ref, xn_ref, lg_ref):
    xn = _rms(x_ref[...], g_ref[...])
    _to_token_tiles(xn_ref, xn)
    lg_ref[...] = jnp.dot(xn, r_ref[...], preferred_element_type=F32, precision=lax.Precision.HIGHEST)


def router(x, gain, w_router, layer, tm=512):
    m, d = x.shape
    e = w_router.shape[2]
    assert d == SUBLANES * LANES
    tm = min(tm, m)
    return pl.pallas_call(
        _router_kernel,
        grid=(m // tm,),
        in_specs=[pl.BlockSpec((tm, d), lambda i: (i, 0)),
                  pl.BlockSpec((1, d), lambda i: (0, 0)),
                  pl.BlockSpec((None, d, e), lambda i: (layer, 0, 0))],
        out_specs=[pl.BlockSpec((tm * SUBLANES, LANES), lambda i: (i, 0)),
                   pl.BlockSpec((tm, e), lambda i: (i, 0))],
        out_shape=[jax.ShapeDtypeStruct((m * SUBLANES, LANES), F32), jax.ShapeDtypeStruct((m, e), F32)],
        compiler_params=_params("parallel"),
        name="router",
    )(x, gain.reshape(1, d), w_router.astype(F32))


def _route_tables(logits, rows):
    n, e = logits.shape
    top_logits, top_idx = lax.top_k(logits, TOP_K)
    gates = jax.nn.softmax(top_logits, axis=-1).reshape(-1)
    e_flat = top_idx.reshape(-1).astype(jnp.int32)
    order = jnp.argsort(e_flat).astype(jnp.int32)
    counts = jnp.sum((e_flat[:, None] == jnp.arange(e, dtype=jnp.int32)[None, :]).astype(jnp.int32), axis=0)
    padded = ((counts + rows - 1) // rows) * rows
    start = jnp.cumsum(counts) - counts
    pend = jnp.cumsum(padded)
    pstart = pend - padded
    n_blocks = -(-(n * TOP_K) // rows) + e
    slot = jnp.arange(n_blocks * rows, dtype=jnp.int32)
    e_s = jnp.minimum(jnp.searchsorted(pend, slot, side='right'), e - 1).astype(jnp.int32)
    rank = slot - pstart[e_s]
    valid = rank < counts[e_s]
    pair = order[jnp.clip(start[e_s] + rank, 0, n * TOP_K - 1)]
    tok = pair // TOP_K
    slot_tok = jnp.where(valid, tok, 0).astype(jnp.int32)
    slot_dst = jnp.where(valid, (pair % TOP_K) * n + tok, TOP_K * n + slot % rows).astype(jnp.int32)
    slot_gate = jnp.where(valid, gates[pair], 0.0).astype(F32)
    block_e = e_s[::rows]
    block_valid = (jnp.arange(n_blocks, dtype=jnp.int32) * rows < pend[-1]).astype(jnp.int32)
    return (block_e, block_valid, slot_tok.reshape(n_blocks, 1, rows),
            slot_dst.reshape(n_blocks, 1, rows), slot_gate.reshape(n_blocks * rows, 1))


def _moe_kernel(be_ref, bv_ref, tok_ref, dst_ref, gate_ref, xn_hbm, wg_ref, wu_ref, wd_ref, y_hbm,
                xg_ref, xb_ref, acc_ref, yb_ref, sem_in, sem_out, *, rows):
    i = pl.program_id(0)
    j = pl.program_id(1)
    valid = bv_ref[i] != 0

    pieces = SUBLANES

    def tile(ref, t):
        return ref.at[pl.ds(pl.multiple_of(t * pieces, pieces), pieces)]

    def row_in(r, t):
        return pltpu.make_async_copy(tile(xn_hbm, t), tile(xg_ref, r), sem_in)

    def row_out(r, t):
        return pltpu.make_async_copy(tile(yb_ref, r), tile(y_hbm, t), sem_out)

    @pl.when(jnp.logical_and(i == 0, j == 0))
    def _():
        yb_ref[...] = jnp.zeros_like(yb_ref)
        n_spare = rows * pieces
        spare = pltpu.make_async_copy(yb_ref, y_hbm.at[pl.ds(y_hbm.shape[0] - n_spare, n_spare)], sem_out)
        spare.start()
        spare.wait()

    @pl.when(jnp.logical_and(valid, j == 0))
    def _():
        def issue(r, c):
            row_in(r, tok_ref[0, 0, r]).start()
            return c

        lax.fori_loop(0, rows, issue, 0, unroll=MOE_ISSUE_UNROLL)
        pltpu.make_async_copy(xg_ref, xg_ref, sem_in).wait()
        for s, piece in enumerate(_from_token_tiles(xg_ref, rows, pieces)):
            xb_ref[:, s * LANES:(s + 1) * LANES] = piece.astype(BF16)
        acc_ref[...] = jnp.zeros_like(acc_ref)

    @pl.when(valid)
    def _():
        acc_ref[...] += _swiglu_step(xb_ref[...], wg_ref, wu_ref, wd_ref)

    @pl.when(jnp.logical_and(valid, j == pl.num_programs(1) - 1))
    def _():
        _to_token_tiles(yb_ref, acc_ref[...] * gate_ref[...])

        def issue(r, c):
            row_out(r, dst_ref[0, 0, r]).start()
            return c

        lax.fori_loop(0, rows, issue, 0, unroll=MOE_ISSUE_UNROLL)
        pltpu.make_async_copy(yb_ref, yb_ref, sem_out).wait()


def moe_ffn(xn, tables, wg, wu, wd, layer, tf=512):
    d, f = wg.shape[2], wg.shape[3]
    n = xn.shape[0] // SUBLANES
    block_e, block_valid, slot_tok, slot_dst, slot_gate = tables
    n_blocks, _, rows = slot_tok.shape
    tf = min(tf, f)
    kern = functools.partial(_moe_kernel, rows=rows)
    grid_spec = pltpu.PrefetchScalarGridSpec(
        num_scalar_prefetch=2,
        grid=(n_blocks, f // tf),
        in_specs=[pl.BlockSpec((1, 1, rows), lambda i, j, be, bv: (i, 0, 0), memory_space=pltpu.SMEM),
                  pl.BlockSpec((1, 1, rows), lambda i, j, be, bv: (i, 0, 0), memory_space=pltpu.SMEM),
                  pl.BlockSpec((rows, 1), lambda i, j, be, bv: (i, 0)),
                  pl.BlockSpec(memory_space=pl.ANY),
                  pl.BlockSpec((None, None, d, tf), lambda i, j, be, bv: (layer, be[i], 0, j)),
                  pl.BlockSpec((None, None, d, tf), lambda i, j, be, bv: (layer, be[i], 0, j)),
                  pl.BlockSpec((None, None, tf, d), lambda i, j, be, bv: (layer, be[i], j, 0))],
        out_specs=pl.BlockSpec(memory_space=pl.ANY),
        scratch_shapes=[pltpu.VMEM((rows * SUBLANES, LANES), F32), pltpu.VMEM((rows, d), BF16),
                        pltpu.VMEM((rows, d), F32), pltpu.VMEM((rows * SUBLANES, LANES), F32),
                        pltpu.SemaphoreType.DMA(()), pltpu.SemaphoreType.DMA(())])
    return pl.pallas_call(
        kern,
        grid_spec=grid_spec,
        out_shape=jax.ShapeDtypeStruct(((TOP_K * n + rows) * SUBLANES, LANES), F32),
        compiler_params=_params("arbitrary", "arbitrary"),
        name="moe_ffn",
    )(block_e, block_valid, slot_tok, slot_dst, slot_gate, xn, wg, wu, wd)


def _combine_tiles(h_ref, y0_ref, y1_ref):
    tm, d = h_ref.shape
    pieces = d // LANES
    y0 = _from_token_tiles(y0_ref, tm, pieces)
    y1 = _from_token_tiles(y1_ref, tm, pieces)
    return jnp.concatenate([y0[s] + y1[s] for s in range(pieces)], axis=1) + h_ref[...]


def _combine_kernel(h_ref, y0_ref, y1_ref, o_ref):
    o_ref[...] = _combine_tiles(h_ref, y0_ref, y1_ref)


def _combine_norm_kernel(h_ref, y0_ref, y1_ref, g_ref, o_ref):
    o_ref[...] = _rms(_combine_tiles(h_ref, y0_ref, y1_ref), g_ref[...])


def moe_combine(h, y, final_gain=None, tm=512):
    n, d = h.shape
    tm = min(tm, n)
    nt = n // tm
    specs = [pl.BlockSpec((tm, d), lambda i: (i, 0)),
             pl.BlockSpec((tm * SUBLANES, LANES), lambda i: (i, 0)),
             pl.BlockSpec((tm * SUBLANES, LANES), lambda i: (nt + i, 0))]
    args = [h, y, y]
    kern = _combine_kernel
    if final_gain is not None:
        specs.append(pl.BlockSpec((1, d), lambda i: (0, 0)))
        args.append(final_gain.reshape(1, d))
        kern = _combine_norm_kernel
    return pl.pallas_call(
        kern,
        grid=(n // tm,),
        in_specs=specs,
        out_specs=pl.BlockSpec((tm, d), lambda i: (i, 0)),
        out_shape=jax.ShapeDtypeStruct((n, d), F32),
        compiler_params=_params("parallel"),
        name="moe_combine",
    )(*args)


def _final_norm_kernel(h_ref, g_ref, o_ref):
    o_ref[...] = _rms(h_ref[...], g_ref[...])


def final_norm(h, gain, tm=512):
    n, d = h.shape
    tm = min(tm, n)
    return pl.pallas_call(
        _final_norm_kernel,
        grid=(n // tm,),
        in_specs=[pl.BlockSpec((tm, d), lambda i: (i, 0)), pl.BlockSpec((1, d), lambda i: (0, 0))],
        out_specs=pl.BlockSpec((tm, d), lambda i: (i, 0)),
        out_shape=jax.ShapeDtypeStruct((n, d), F32),
        compiler_params=_params("parallel"),
        name="final_norm",
    )(h, gain.reshape(1, d))


def kernel(x, norm_mix, norm_ffn, norm_final, ret_w_in, ret_gn, ret_w_out, s5_w_in, s5_a_re, s5_a_im, s5_log_step, s5_b_re, s5_b_im, s5_c_re, s5_c_im, s5_d, s5_w_out, moba_w_in, moba_w_out, ffn_w_gate, ffn_w_up, ffn_w_down, moe_router, moe_w_gate, moe_w_up, moe_w_down):
    batch, seq, d = x.shape
    depth = norm_mix.shape[0]
    n = batch * seq
    h = x.reshape(n, d).astype(F32)
    i_ret = i_s5 = i_moba = i_dense = i_moe = 0
    out = None
    for i in range(depth):
        mixer = i % N_MIXERS
        if mixer == 0:
            proj = norm_matmul(h, norm_mix[i], ret_w_in, i_ret, BF16)
            y = retention_core(proj, ret_gn[i_ret], batch, seq)
            h = matmul_residual(y, ret_w_out, i_ret, h)
            i_ret += 1
        elif mixer == 1:
            sub = math.gcd(seq, S5_SUBCHUNK)
            u = norm_matmul(h, norm_mix[i], s5_w_in, i_s5, F32)
            ops = _s5_operators(s5_a_re[i_s5], s5_a_im[i_s5], s5_log_step[i_s5], s5_b_re[i_s5],
                                s5_b_im[i_s5], s5_c_re[i_s5], s5_c_im[i_s5], sub)
            y = s5_core(u, s5_d[i_s5], ops, batch, seq, sub)
            h = matmul_glu_residual(y, s5_w_out, i_s5, h)
            i_s5 += 1
        else:
            qkv = norm_matmul(h, norm_mix[i], moba_w_in, i_moba, BF16)
            o = moba_core(qkv, batch, seq)
            h = matmul_residual(o, moba_w_out, i_moba, h)
            i_moba += 1
        last = i == depth - 1
        if i % 2 == 0:
            h = dense_ffn(h, norm_ffn[i], ffn_w_gate, ffn_w_up, ffn_w_down, i_dense)
            i_dense += 1
            if last:
                out = final_norm(h, norm_final)
        else:
            xn, logits = router(h, norm_ffn[i], moe_router, i_moe)
            tables = _route_tables(logits, min(MOE_ROWS, n))
            y = moe_ffn(xn, tables, moe_w_gate, moe_w_up, moe_w_down, i_moe)
            i_moe += 1
            if last:
                out = moe_combine(h, y, norm_final)
            else:
                h = moe_combine(h, y)
    return out.reshape(batch, seq, d).astype(x.dtype)
```

```python
import functools
import math

import numpy as np
import jax
import jax.numpy as jnp
from jax import lax
from jax.experimental import pallas as pl
from jax.experimental.pallas import tpu as pltpu

F32 = jnp.float32
BF16 = jnp.bfloat16

NORM_EPS = 1e-6
NEG_INF = -1e30
N_MIXERS = 3

RET_HEADS = 4
RET_CHUNK = 256
S5_GROUP = 16
S5_LANE_GROUPS = 8
S5_SUBCHUNK = 16
S5_ROWS = 128
S5_DT_MIN = 1e-3
MOBA_HEADS = 8
MOBA_BLOCK = 256
MOBA_TOPK = 3
MOBA_GROUP = 4
MOBA_QUERY_BLOCKS = 2
TOP_K = 2
MOE_ROWS = 1024
MOE_ISSUE_UNROLL = 8

V7X_VMEM_LIMIT_BYTES = 56 * 1024 * 1024
LANES = 128
SUBLANES = 8


def _params(*sem):
    return pltpu.CompilerParams(dimension_semantics=sem, vmem_limit_bytes=V7X_VMEM_LIMIT_BYTES)


def _rms(x, gain):
    return x * lax.rsqrt(jnp.mean(x * x, axis=-1, keepdims=True) + NORM_EPS) * gain


def _dot(a, b):
    return jnp.dot(a, b, preferred_element_type=F32)


def _dot_nt(a, b):
    return lax.dot_general(a, b, (((1,), (1,)), ((), ())), preferred_element_type=F32)


def _dot_tn(a, b):
    return lax.dot_general(a, b, (((0,), (0,)), ((), ())), preferred_element_type=F32)


def _norm_matmul_kernel(x_ref, g_ref, w_ref, o_ref, xn_ref):
    @pl.when(pl.program_id(1) == 0)
    def _():
        xn_ref[...] = _rms(x_ref[...], g_ref[...]).astype(BF16)

    o_ref[...] = _dot(xn_ref[...], w_ref[...]).astype(o_ref.dtype)


def norm_matmul(x, gain, w, layer, out_dtype, tm=1024, tn=1024):
    m, d = x.shape
    n = w.shape[2]
    tm, tn = min(tm, m), min(tn, n)
    return pl.pallas_call(
        _norm_matmul_kernel,
        grid=(m // tm, n // tn),
        in_specs=[pl.BlockSpec((tm, d), lambda i, j: (i, 0)),
                  pl.BlockSpec((1, d), lambda i, j: (0, 0)),
                  pl.BlockSpec((d, tn), lambda i, j: (0, j))],
        out_specs=pl.BlockSpec((tm, tn), lambda i, j: (i, j)),
        out_shape=jax.ShapeDtypeStruct((m, n), out_dtype),
        scratch_shapes=[pltpu.VMEM((tm, d), BF16)],
        compiler_params=_params("parallel", "arbitrary"),
        name="norm_matmul",
    )(x, gain.reshape(1, d), w[layer].astype(BF16))


def _matmul_res_kernel(a_ref, w_ref, r_ref, o_ref, wb_ref):
    @pl.when(pl.program_id(0) == 0)
    def _():
        wb_ref[...] = w_ref[...].astype(BF16)

    o_ref[...] = r_ref[...] + _dot(a_ref[...].astype(BF16), wb_ref[...])


def matmul_residual(a, w, layer, res, tm=512):
    m, k = a.shape
    n = w.shape[2]
    tm = min(tm, m)
    return pl.pallas_call(
        _matmul_res_kernel,
        grid=(m // tm,),
        in_specs=[pl.BlockSpec((tm, k), lambda i: (i, 0)),
                  pl.BlockSpec((None, k, n), lambda i: (layer, 0, 0)),
                  pl.BlockSpec((tm, n), lambda i: (i, 0))],
        out_specs=pl.BlockSpec((tm, n), lambda i: (i, 0)),
        out_shape=jax.ShapeDtypeStruct((m, n), F32),
        scratch_shapes=[pltpu.VMEM((k, n), BF16)],
        compiler_params=_params("arbitrary"),
        name="matmul_residual",
    )(a, w, res)


def _matmul_glu_res_kernel(a_ref, wa_ref, wb_ref, r_ref, o_ref, wa_bf_ref, wb_bf_ref):
    @pl.when(pl.program_id(0) == 0)
    def _():
        wa_bf_ref[...] = wa_ref[...].astype(BF16)
        wb_bf_ref[...] = wb_ref[...].astype(BF16)

    a = a_ref[...].astype(BF16)
    za = _dot(a, wa_bf_ref[...])
    zb = _dot(a, wb_bf_ref[...])
    o_ref[...] = r_ref[...] + za * jax.nn.sigmoid(zb)


def matmul_glu_residual(a, w, layer, res, tm=512):
    m, k = a.shape
    n = w.shape[2] // 2
    tm = min(tm, m)
    return pl.pallas_call(
        _matmul_glu_res_kernel,
        grid=(m // tm,),
        in_specs=[pl.BlockSpec((tm, k), lambda i: (i, 0)),
                  pl.BlockSpec((None, k, n), lambda i: (layer, 0, 0)),
                  pl.BlockSpec((None, k, n), lambda i: (layer, 0, 1)),
                  pl.BlockSpec((tm, n), lambda i: (i, 0))],
        out_specs=pl.BlockSpec((tm, n), lambda i: (i, 0)),
        out_shape=jax.ShapeDtypeStruct((m, n), F32),
        scratch_shapes=[pltpu.VMEM((k, n), BF16), pltpu.VMEM((k, n), BF16)],
        compiler_params=_params("arbitrary"),
        name="matmul_glu_residual",
    )(a, w, w, res)


def _swiglu_step(xn, wg_ref, wu_ref, wd_ref):
    g = _dot(xn, wg_ref[...].astype(BF16))
    u = _dot(xn, wu_ref[...].astype(BF16))
    a = (g * jax.nn.sigmoid(g) * u).astype(BF16)
    return _dot(a, wd_ref[...].astype(BF16))


def _ffn_kernel(x_ref, g_ref, wg_ref, wu_ref, wd_ref, o_ref, xn_ref, acc_ref):
    j = pl.program_id(1)

    @pl.when(j == 0)
    def _():
        xn_ref[...] = _rms(x_ref[...], g_ref[...]).astype(BF16)
        acc_ref[...] = jnp.zeros_like(acc_ref)

    acc_ref[...] += _swiglu_step(xn_ref[...], wg_ref, wu_ref, wd_ref)

    @pl.when(j == pl.num_programs(1) - 1)
    def _():
        o_ref[...] = x_ref[...] + acc_ref[...]


def dense_ffn(x, gain, wg, wu, wd, layer, tm=1024, tf=512):
    m, d = x.shape
    f = wg.shape[2]
    tm, tf = min(tm, m), min(tf, f)
    return pl.pallas_call(
        _ffn_kernel,
        grid=(m // tm, f // tf),
        in_specs=[pl.BlockSpec((tm, d), lambda i, j: (i, 0)),
                  pl.BlockSpec((1, d), lambda i, j: (0, 0)),
                  pl.BlockSpec((None, d, tf), lambda i, j: (layer, 0, j)),
                  pl.BlockSpec((None, d, tf), lambda i, j: (layer, 0, j)),
                  pl.BlockSpec((None, tf, d), lambda i, j: (layer, j, 0))],
        out_specs=pl.BlockSpec((tm, d), lambda i, j: (i, 0)),
        out_shape=jax.ShapeDtypeStruct((m, d), F32),
        scratch_shapes=[pltpu.VMEM((tm, d), BF16), pltpu.VMEM((tm, d), F32)],
        compiler_params=_params("parallel", "arbitrary"),
        name="dense_ffn",
    )(x, gain.reshape(1, d), wg, wu, wd)


def _retention_kernel(q_ref, k_ref, v_ref, g_ref, dec_ref, qd_ref, kd_ref, gn_ref, o_ref, r_ref,
                      *, heads, dk, dv, chunk_decay):
    @pl.when(pl.program_id(1) == 0)
    def _():
        r_ref[...] = jnp.zeros_like(r_ref)

    for h in range(heads):
        qh = q_ref[:, h * dk:(h + 1) * dk]
        kh = k_ref[:, h * dk:(h + 1) * dk]
        vh = v_ref[:, h * dv:(h + 1) * dv]
        s = _dot_nt(qh, kh) * dec_ref[h]
        y = _dot(s.astype(BF16), vh)
        y = y + _dot(qh, r_ref[h].astype(BF16)) * qd_ref[h]
        kd = (kh.astype(F32) * kd_ref[h]).astype(BF16)
        r_ref[h] = r_ref[h] * chunk_decay[h] + _dot_tn(kd, vh)
        mu = jnp.mean(y, axis=-1, keepdims=True)
        yc = y - mu
        var = jnp.mean(yc * yc, axis=-1, keepdims=True)
        yn = yc * lax.rsqrt(var + NORM_EPS) * gn_ref[:, h * dv:(h + 1) * dv]
        gate = g_ref[:, h * dv:(h + 1) * dv].astype(F32)
        o_ref[:, h * dv:(h + 1) * dv] = (yn * (gate * jax.nn.sigmoid(gate))).astype(o_ref.dtype)


def retention_core(proj, gn_gain, batch, seq):
    d = proj.shape[1] // 6
    heads = RET_HEADS
    dk, dv = d // heads, 2 * d // heads
    c = math.gcd(seq, RET_CHUNK)
    nc = seq // c
    log_gamma = np.log1p(-np.exp2(-5.0 - np.arange(heads, dtype=np.float64)))
    idx = np.arange(c, dtype=np.float64)
    diff = idx[:, None] - idx[None, :]
    scale = dk ** -0.5
    decay = np.where(diff >= 0, np.exp(log_gamma[:, None, None] * np.maximum(diff, 0.0)), 0.0) * scale
    q_decay = np.exp(log_gamma[:, None] * (idx + 1.0))[:, :, None]
    k_decay = np.exp(log_gamma[:, None] * (c - 1.0 - idx))[:, :, None] * scale
    chunk_decay = tuple(float(v) for v in np.exp(log_gamma * c))
    kern = functools.partial(_retention_kernel, heads=heads, dk=dk, dv=dv, chunk_decay=chunk_decay)
    row = lambda b, n: b * nc + n
    return pl.pallas_call(
        kern,
        grid=(batch, nc),
        in_specs=[pl.BlockSpec((c, d), lambda b, n: (row(b, n), 0)),
                  pl.BlockSpec((c, d), lambda b, n: (row(b, n), 1)),
                  pl.BlockSpec((c, 2 * d), lambda b, n: (row(b, n), 1)),
                  pl.BlockSpec((c, 2 * d), lambda b, n: (row(b, n), 2)),
                  pl.BlockSpec((heads, c, c), lambda b, n: (0, 0, 0)),
                  pl.BlockSpec((heads, c, 1), lambda b, n: (0, 0, 0)),
                  pl.BlockSpec((heads, c, 1), lambda b, n: (0, 0, 0)),
                  pl.BlockSpec((1, 2 * d), lambda b, n: (0, 0))],
        out_specs=pl.BlockSpec((c, 2 * d), lambda b, n: (row(b, n), 0)),
        out_shape=jax.ShapeDtypeStruct((batch * seq, 2 * d), BF16),
        scratch_shapes=[pltpu.VMEM((heads, dk, dv), F32)],
        compiler_params=_params("parallel", "arbitrary"),
        name="retention_core",
    )(proj, proj, proj, proj, jnp.asarray(decay, F32), jnp.asarray(q_decay, F32),
      jnp.asarray(k_decay, F32), gn_gain.reshape(1, 2 * d).astype(F32))


def _gelu_tanh(x):
    return 0.5 * x * (1.0 + jnp.tanh(math.sqrt(2.0 / math.pi) * (x + 0.044715 * (x * x * x))))


def _s5_expand(tc_ref, sc_ref, qc_ref, es_ref, eq_ref, m1_ref, m2_ref, *, sub, ga, half):
    width = sub * LANES
    n_ch = LANES // ga
    p = half // ga

    def keep_own_group(x, row_group, col_group):
        return jnp.where(row_group == col_group, x, 0.0).astype(BF16)

    lag = _dot(tc_ref[0], eq_ref[...])
    ra = lax.broadcasted_iota(jnp.int32, lag.shape, 0) // n_ch
    cb = (lax.broadcasted_iota(jnp.int32, lag.shape, 1) % LANES) // n_ch
    lag = keep_own_group(lag, ra, cb)
    for s in range(sub):
        rs = slice(s * LANES, (s + 1) * LANES)
        if s:
            m1_ref[rs, 0:s * LANES] = jnp.zeros((LANES, s * LANES), BF16)
        m1_ref[rs, s * LANES:width] = lag[:, 0:(sub - s) * LANES]
        sx = _dot(sc_ref[0, rs, :], es_ref[...])
        ra = lax.broadcasted_iota(jnp.int32, sx.shape, 0) // n_ch
        cb = (lax.broadcasted_iota(jnp.int32, sx.shape, 1) % half) // p
        m1_ref[rs, width:width + 2 * half] = keep_own_group(sx, ra, cb)
    for r in range(2 * ga):
        qx = _dot(qc_ref[0, r * p:(r + 1) * p, :], eq_ref[...])
        cb = (lax.broadcasted_iota(jnp.int32, qx.shape, 1) % LANES) // n_ch
        m2_ref[r * p:(r + 1) * p, :] = keep_own_group(qx, r % ga, cb)


def _s5_kernel(u_ref, d_ref, tc_ref, sc_ref, qc_ref, es_ref, eq_ref, al_ref, o_ref,
               m1_ref, m2_ref, xl_ref, xp_ref, st_ref, *, sub, rows, steps_per_seq, ga):
    half = st_ref.shape[1] // 2
    width = sub * LANES

    @pl.when(pl.program_id(1) == 0)
    def _():
        _s5_expand(tc_ref, sc_ref, qc_ref, es_ref, eq_ref, m1_ref, m2_ref, sub=sub, ga=ga, half=half)

    @pl.when(pl.program_id(1) % steps_per_seq == 0)
    def _():
        st_ref[...] = jnp.zeros_like(st_ref)

    us = [u_ref[pl.ds(t, rows, stride=sub), :] for t in range(sub)]
    ucat = jnp.concatenate([u.astype(BF16) for u in us], axis=1)
    y_all = _dot(ucat, m1_ref[...])
    xl_ref[...] = y_all[:, width:]

    a_re = al_ref[0, 0:1, :]
    a_im = al_ref[0, 1:2, :]

    def step(c, carry):
        xr, xi = carry
        xp_ref[pl.ds(c, 1), 0:half] = xr
        xp_ref[pl.ds(c, 1), half:2 * half] = xi
        lr = xl_ref[pl.ds(c, 1), 0:half]
        li = xl_ref[pl.ds(c, 1), half:2 * half]
        return a_re * xr - a_im * xi + lr, a_re * xi + a_im * xr + li

    xr, xi = lax.fori_loop(0, rows, step, (st_ref[0:1, 0:half], st_ref[0:1, half:2 * half]))
    st_ref[0:1, 0:half] = xr
    st_ref[0:1, half:2 * half] = xi

    y_cross = _dot(xp_ref[...].astype(BF16), m2_ref[...])
    skip = d_ref[...]
    for t in range(sub):
        y = y_all[:, t * LANES:(t + 1) * LANES] + y_cross[:, t * LANES:(t + 1) * LANES] + skip * us[t]
        o_ref[pl.ds(t, rows, stride=sub), :] = _gelu_tanh(y)


def _s5_operators(a_re, a_im, log_step, b_re, b_im, c_re, c_im, sub):
    g, p = a_re.shape
    n = b_re.shape[2]
    ga = S5_LANE_GROUPS
    j = g // ga
    dt = jnp.exp(log_step.astype(F32))[:, None]
    ar, ai = a_re.astype(F32), a_im.astype(F32)
    mag = jnp.exp(ar * dt)
    abar_re = mag * jnp.cos(ai * dt)
    abar_im = mag * jnp.sin(ai * dt)
    den = ar * ar + ai * ai
    nr, ni = abar_re - 1.0, abar_im
    f_re = (nr * ar + ni * ai) / den
    f_im = (ni * ar - nr * ai) / den
    br, bi = b_re.astype(F32), b_im.astype(F32)
    bb_re = f_re[..., None] * br - f_im[..., None] * bi
    bb_im = f_re[..., None] * bi + f_im[..., None] * br
    cr, ci = c_re.astype(F32), c_im.astype(F32)
    tau = jnp.arange(sub + 1, dtype=F32)[:, None, None]
    pw_mag = jnp.exp(tau * (ar * dt)[None])
    pw_re = pw_mag * jnp.cos(tau * (ai * dt)[None])
    pw_im = pw_mag * jnp.sin(tau * (ai * dt)[None])
    hp = lax.Precision.HIGHEST
    ab_re = pw_re[..., None] * bb_re[None] - pw_im[..., None] * bb_im[None]
    ab_im = pw_re[..., None] * bb_im[None] + pw_im[..., None] * bb_re[None]
    lag = (jnp.einsum('tgpm,gnp->tgmn', ab_re[:sub], cr, precision=hp)
           - jnp.einsum('tgpm,gnp->tgmn', ab_im[:sub], ci, precision=hp))
    tc = lag.reshape(sub, j, ga, n, n).transpose(1, 2, 3, 0, 4).reshape(j, ga * n, sub * n)
    rev = sub - 1 - jnp.arange(sub)
    sc = jnp.stack([ab_re[rev], ab_im[rev]], axis=0).reshape(2, sub, j, ga, p, n)
    sc = sc.transpose(2, 1, 3, 5, 0, 4).reshape(j, sub * ga * n, 2 * p)
    q_re = cr[None] * pw_re[1:][:, :, None, :] - ci[None] * pw_im[1:][:, :, None, :]
    q_im = -(cr[None] * pw_im[1:][:, :, None, :] + ci[None] * pw_re[1:][:, :, None, :])
    qc = jnp.stack([q_re, q_im], axis=0).reshape(2, sub, j, ga, n, p)
    qc = qc.transpose(2, 0, 3, 5, 1, 4).reshape(j, 2 * ga * p, sub * n)
    al = jnp.stack([pw_re[sub].reshape(j, ga * p), pw_im[sub].reshape(j, ga * p)], axis=1)
    return tc.astype(BF16), sc.astype(BF16), qc.astype(BF16), al


def _s5_spread_matrices(sub, ga, n, p):
    half = ga * p
    r = np.arange(2 * p)
    c = np.arange(2 * half)
    es = (r[:, None] // p == c[None, :] // half) & (r[:, None] % p == c[None, :] % p)
    r = np.arange(sub * n)
    c = np.arange(sub * ga * n)
    eq = (r[:, None] // n == c[None, :] // (ga * n)) & (r[:, None] % n == c[None, :] % n)
    return jnp.asarray(es, BF16), jnp.asarray(eq, BF16)


def s5_core(u, d_skip, ops, batch, seq, sub):
    tc, sc, qc, al = ops
    n_tok, w = u.shape
    j = tc.shape[0]
    half = al.shape[2]
    ga = S5_LANE_GROUPS
    es, eq = _s5_spread_matrices(sub, ga, LANES // ga, half // ga)
    rows = min(S5_ROWS, seq // sub)
    step_tok = rows * sub
    steps_per_seq = seq // step_tok
    width = sub * LANES
    kern = functools.partial(_s5_kernel, sub=sub, rows=rows, steps_per_seq=steps_per_seq, ga=ga)
    tile = lambda a: pl.BlockSpec((1,) + a.shape[1:], lambda jj, i: (jj, 0, 0))
    whole = lambda a: pl.BlockSpec(a.shape, lambda jj, i: (0, 0))
    return pl.pallas_call(
        kern,
        grid=(j, n_tok // step_tok),
        in_specs=[pl.BlockSpec((step_tok, LANES), lambda jj, i: (i, jj)),
                  pl.BlockSpec((1, LANES), lambda jj, i: (0, jj)),
                  tile(tc), tile(sc), tile(qc), whole(es), whole(eq), tile(al)],
        out_specs=pl.BlockSpec((step_tok, LANES), lambda jj, i: (i, jj)),
        out_shape=jax.ShapeDtypeStruct((n_tok, w), F32),
        scratch_shapes=[pltpu.VMEM((width, width + 2 * half), BF16), pltpu.VMEM((2 * half, width), BF16),
                        pltpu.VMEM((rows, 2 * half), F32), pltpu.VMEM((rows, 2 * half), F32),
                        pltpu.VMEM((8, 2 * half), F32)],
        compiler_params=_params("parallel", "arbitrary"),
        name="s5_core",
    )(u, d_skip.reshape(1, w).astype(F32), tc, sc, qc, es, eq, al)


MOBA_AUX_CONSTS = 6


def _moba_kernel(cst_ref, q_ref, k_ref, v_ref, o_ref, km_ref, ka_ref, vt_ref, sa_ref, sb_ref,
                 *, blk, topk, scale, group, qblocks):
    h = pl.program_id(1)
    i = pl.program_id(2)
    first = i * qblocks
    nb = k_ref.shape[0] // blk
    hd = q_ref.shape[1]
    aux = ka_ref.shape[2] - hd
    nc = MOBA_AUX_CONSTS

    @pl.when(i == 0)
    def _():
        km_ref[...] = jnp.zeros_like(km_ref)
        jl = lax.broadcasted_iota(jnp.int32, (blk, aux), 0)
        lane = lax.broadcasted_iota(jnp.int32, (blk, aux), 1)

        def fill(n, c):
            st = pl.multiple_of(n * blk, blk)
            kb = k_ref[pl.ds(st, blk), :]
            km_ref[pl.ds(n + nc, 1), :] = jnp.mean(kb.astype(F32), axis=0, keepdims=True)
            pat = jnp.where(lane < nc // 2, n, jnp.where(lane < nc, jl, (lane - nc == n).astype(jnp.int32)))
            ka_ref[n, :, 0:hd] = kb
            ka_ref[n, :, hd:hd + aux] = pat.astype(F32).astype(BF16)
            vt_ref[n] = v_ref[pl.ds(st, blk), :].astype(F32).T.astype(BF16)
            return c

        lax.fori_loop(0, nb, fill, 0)

    q = q_ref[...]
    gate = lax.dot_general(q.astype(F32), km_ref[...], (((1,), (1,)), ((), ())),
                           preferred_element_type=F32, precision=lax.Precision.HIGHEST)
    lane = lax.broadcasted_iota(jnp.int32, gate.shape, 1)
    own = first + lax.broadcasted_iota(jnp.int32, gate.shape, 0) // blk
    gate = jnp.where(jnp.logical_and(lane >= nc, lane - nc < own), gate, NEG_INF)
    chosen = jnp.zeros(gate.shape, jnp.bool_)
    lane_f = lane.astype(F32)
    for _ in range(topk):
        mx = jnp.max(gate, axis=1, keepdims=True)
        idx = jnp.min(jnp.where(gate == mx, lane_f, float(2 ** 20)), axis=1, keepdims=True)
        hit = lane_f == idx
        chosen = jnp.logical_or(chosen, jnp.logical_and(hit, mx > 0.5 * NEG_INF))
        gate = jnp.where(hit, NEG_INF, gate)

    qx = jnp.where(chosen, 0.0, NEG_INF)
    for c in range(nc):
        qx = jnp.where(lane == c, cst_ref[h, c], qx)
    qa = jnp.concatenate([q, qx.astype(BF16)], axis=1)
    a_full = cst_ref[h, 0] + cst_ref[h, 1] + cst_ref[h, 2]
    c_full = cst_ref[h, 3] + cst_ref[h, 4] + cst_ref[h, 5]

    jk = lax.broadcasted_iota(jnp.int32, (blk, blk), 0)
    jq = lax.broadcasted_iota(jnp.int32, (blk, blk), 1)
    c2 = scale * math.log2(math.e)
    m0, l0, acc0 = [], [], []
    for w in range(qblocks):
        bw = first + w
        st = _dot_nt(ka_ref[bw, :, 0:hd], q[w * blk:(w + 1) * blk])
        st = st + (c_full * jk.astype(F32) + a_full * bw.astype(F32))
        st = jnp.where(jk <= jq, st, NEG_INF)
        mw = jnp.max(st, axis=0, keepdims=True)
        p = jnp.exp2((st - mw) * c2)
        m0.append(mw)
        l0.append(jnp.sum(p, axis=0, keepdims=True))
        acc0.append(_dot(vt_ref[bw], p.astype(BF16)))
    m0, l0, acc0 = (jnp.concatenate(x, axis=1) for x in (m0, l0, acc0))

    n_groups = nb // group

    half = group // 2

    def scores(g, s_ref):
        for u in range(2):
            ka = ka_ref[pl.ds(g * group + u * half, half)].reshape(half * blk, hd + aux)
            s_ref[u * half * blk:(u + 1) * half * blk, :] = _dot_nt(ka, qa)

    def softmax(st, m, l):
        m_new = jnp.maximum(m, jnp.max(st, axis=0, keepdims=True))
        alpha = jnp.exp2((m - m_new) * c2)
        p = jnp.exp2((st - m_new) * c2)
        return m_new, alpha * l + jnp.sum(p, axis=0, keepdims=True), alpha, p.astype(BF16)

    def weighted_values(g, pb):
        pv = _dot(vt_ref[g * group], pb[0:blk])
        for u in range(1, group):
            pv = pv + _dot(vt_ref[g * group + u], pb[u * blk:(u + 1) * blk])
        return pv

    scores(0, sa_ref)
    past = first + qblocks - 1

    def body(t, carry):
        m, l, acc = carry
        g0 = 2 * t
        g1 = g0 + 1
        g2 = jnp.minimum(g0 + 2, n_groups - 1)
        scores(g1, sb_ref)
        m, l, alpha_a, pa = softmax(sa_ref[...], m, l)
        acc = alpha_a * acc + weighted_values(g0, pa)
        scores(g2, sa_ref)
        m, l, alpha_b, pb = softmax(sb_ref[...], m, l)
        return m, l, alpha_b * acc + weighted_values(g1, pb)

    m, l, acc = lax.fori_loop(0, (past + 2 * group - 1) // (2 * group), body, (m0, l0, acc0))
    o_ref[...] = (acc / l).T.astype(o_ref.dtype)


def moba_core(qkv, batch, seq):
    d = qkv.shape[1] // 3
    heads = MOBA_HEADS
    hd = d // heads
    blk = MOBA_BLOCK
    assert seq % blk == 0
    nb = seq // blk
    topk = max(1, min(MOBA_TOPK, nb - 1))
    group = math.gcd(nb, MOBA_GROUP)
    aux = LANES
    assert MOBA_AUX_CONSTS + nb <= aux and (nb // group) % 2 == 0
    scale = hd ** -0.5
    slopes = np.exp2(-8.0 * (np.arange(heads, dtype=np.float64) + 1.0) / heads)
    consts = []
    for val in (slopes * blk / scale, slopes / scale):
        rest = jnp.asarray(val, F32)
        for _ in range(MOBA_AUX_CONSTS // 2):
            piece = rest.astype(BF16).astype(F32)
            consts.append(piece)
            rest = rest - piece
    cst = jnp.stack(consts, axis=1)
    qblocks = math.gcd(nb, MOBA_QUERY_BLOCKS)
    nq = nb // qblocks
    kern = functools.partial(_moba_kernel, blk=blk, topk=topk, scale=scale, group=group, qblocks=qblocks)
    return pl.pallas_call(
        kern,
        grid=(batch, heads, nq),
        in_specs=[pl.BlockSpec(memory_space=pltpu.SMEM),
                  pl.BlockSpec((qblocks * blk, hd), lambda b, h, i: (b * nq + i, h)),
                  pl.BlockSpec((seq, hd), lambda b, h, i: (b, heads + h)),
                  pl.BlockSpec((seq, hd), lambda b, h, i: (b, 2 * heads + h))],
        out_specs=pl.BlockSpec((qblocks * blk, hd), lambda b, h, i: (b * nq + i, h)),
        out_shape=jax.ShapeDtypeStruct((batch * seq, d), BF16),
        scratch_shapes=[pltpu.VMEM((aux, hd), F32), pltpu.VMEM((nb, blk, hd + aux), BF16),
                        pltpu.VMEM((nb, hd, blk), BF16),
                        pltpu.VMEM((group * blk, qblocks * blk), F32),
                        pltpu.VMEM((group * blk, qblocks * blk), F32)],
        compiler_params=_params("parallel", "parallel", "arbitrary"),
        name="moba_core",
    )(cst, qkv, qkv, qkv)


def _to_token_tiles(ref, x):
    pieces = x.shape[1] // LANES
    for s in range(pieces):
        ref[pl.ds(s, x.shape[0], stride=pieces), :] = x[:, s * LANES:(s + 1) * LANES]


def _from_token_tiles(ref, tokens, pieces):
    return [ref[pl.ds(s, tokens, stride=pieces), :] for s in range(pieces)]


def _router_kernel(x_ref, g_ref, r_ref, xn_ref, lg_ref):
    xn = _rms(x_ref[...], g_ref[...])
    _to_token_tiles(xn_ref, xn)
    lg_ref[...] = jnp.dot(xn, r_ref[...], preferred_element_type=F32, precision=lax.Precision.HIGHEST)


def router(x, gain, w_router, layer, tm=512):
    m, d = x.shape
    e = w_router.shape[2]
    assert d == SUBLANES * LANES
    tm = min(tm, m)
    return pl.pallas_call(
        _router_kernel,
        grid=(m // tm,),
        in_specs=[pl.BlockSpec((tm, d), lambda i: (i, 0)),
                  pl.BlockSpec((1, d), lambda i: (0, 0)),
                  pl.BlockSpec((None, d, e), lambda i: (layer, 0, 0))],
        out_specs=[pl.BlockSpec((tm * SUBLANES, LANES), lambda i: (i, 0)),
                   pl.BlockSpec((tm, e), lambda i: (i, 0))],
        out_shape=[jax.ShapeDtypeStruct((m * SUBLANES, LANES), F32), jax.ShapeDtypeStruct((m, e), F32)],
        compiler_params=_params("parallel"),
        name="router",
    )(x, gain.reshape(1, d), w_router.astype(F32))


def _route_tables(logits, rows):
    n, e = logits.shape
    top_logits, top_idx = lax.top_k(logits, TOP_K)
    gates = jax.nn.softmax(top_logits, axis=-1).reshape(-1)
    e_flat = top_idx.reshape(-1).astype(jnp.int32)
    order = jnp.argsort(e_flat).astype(jnp.int32)
    counts = jnp.sum((e_flat[:, None] == jnp.arange(e, dtype=jnp.int32)[None, :]).astype(jnp.int32), axis=0)
    padded = ((counts + rows - 1) // rows) * rows
    start = jnp.cumsum(counts) - counts
    pend = jnp.cumsum(padded)
    pstart = pend - padded
    n_blocks = -(-(n * TOP_K) // rows) + e
    slot = jnp.arange(n_blocks * rows, dtype=jnp.int32)
    e_s = jnp.minimum(jnp.searchsorted(pend, slot, side='right'), e - 1).astype(jnp.int32)
    rank = slot - pstart[e_s]
    valid = rank < counts[e_s]
    pair = order[jnp.clip(start[e_s] + rank, 0, n * TOP_K - 1)]
    tok = pair // TOP_K
    slot_tok = jnp.where(valid, tok, 0).astype(jnp.int32)
    slot_dst = jnp.where(valid, (pair % TOP_K) * n + tok, TOP_K * n + slot % rows).astype(jnp.int32)
    slot_gate = jnp.where(valid, gates[pair], 0.0).astype(F32)
    block_e = e_s[::rows]
    block_valid = (jnp.arange(n_blocks, dtype=jnp.int32) * rows < pend[-1]).astype(jnp.int32)
    return (block_e, block_valid, slot_tok.reshape(n_blocks, 1, rows),
            slot_dst.reshape(n_blocks, 1, rows), slot_gate.reshape(n_blocks * rows, 1))


def _moe_kernel(be_ref, bv_ref, tok_ref, tok_next_ref, dst_ref, gate_ref, xn_hbm, wg_ref, wu_ref, wd_ref, y_hbm,
                xg_ref, xb_ref, acc_ref, yb_ref, sem_in, sem_out, *, rows):
    i = pl.program_id(0)
    j = pl.program_id(1)
    last_j = pl.num_programs(1) - 1
    valid = bv_ref[i] != 0
    nxt = jnp.minimum(i + 1, pl.num_programs(0) - 1)
    next_valid = jnp.logical_and(i + 1 < pl.num_programs(0), bv_ref[nxt] != 0)
    pieces = SUBLANES

    def tile(ref, t):
        return ref.at[pl.ds(pl.multiple_of(t * pieces, pieces), pieces)]

    def gather(table_ref, slot):
        def issue(r, c):
            pltpu.make_async_copy(tile(xn_hbm, table_ref[0, 0, r]), tile(xg_ref.at[slot], r), sem_in.at[slot]).start()
            return c

        lax.fori_loop(0, rows, issue, 0, unroll=MOE_ISSUE_UNROLL)

    def gather_wait(slot):
        pltpu.make_async_copy(xg_ref.at[slot], xg_ref.at[slot], sem_in.at[slot]).wait()

    def scatter():
        def issue(r, c):
            pltpu.make_async_copy(tile(yb_ref, r), tile(y_hbm, dst_ref[0, 0, r]), sem_out).start()
            return c

        lax.fori_loop(0, rows, issue, 0, unroll=MOE_ISSUE_UNROLL)

    def scatter_wait():
        pltpu.make_async_copy(yb_ref, yb_ref, sem_out).wait()

    @pl.when(jnp.logical_and(i == 0, j == 0))
    def _():
        yb_ref[...] = jnp.zeros_like(yb_ref)
        n_spare = rows * pieces
        spare = pltpu.make_async_copy(yb_ref, y_hbm.at[pl.ds(y_hbm.shape[0] - n_spare, n_spare)], sem_out)
        spare.start()
        spare.wait()

        @pl.when(valid)
        def _():
            gather(tok_ref, 0)

    @pl.when(jnp.logical_and(valid, j == 0))
    def _():
        for slot in range(2):
            @pl.when(i % 2 == slot)
            def _():
                gather_wait(slot)
                for s, piece in enumerate(_from_token_tiles(xg_ref.at[slot], rows, pieces)):
                    xb_ref[:, s * LANES:(s + 1) * LANES] = piece.astype(BF16)

                @pl.when(next_valid)
                def _():
                    gather(tok_next_ref, 1 - slot)

        acc_ref[...] = jnp.zeros_like(acc_ref)

    @pl.when(valid)
    def _():
        acc_ref[...] += _swiglu_step(xb_ref[...], wg_ref, wu_ref, wd_ref)

    @pl.when(jnp.logical_and(valid, j == last_j))
    def _():
        @pl.when(i > 0)
        def _():
            scatter_wait()

        _to_token_tiles(yb_ref, acc_ref[...] * gate_ref[...])
        scatter()

        @pl.when(jnp.logical_not(next_valid))
        def _():
            scatter_wait()


def moe_ffn(xn, tables, wg, wu, wd, layer, tf=512):
    d, f = wg.shape[2], wg.shape[3]
    n = xn.shape[0] // SUBLANES
    block_e, block_valid, slot_tok, slot_dst, slot_gate = tables
    n_blocks, _, rows = slot_tok.shape
    tf = min(tf, f)
    kern = functools.partial(_moe_kernel, rows=rows)
    grid_spec = pltpu.PrefetchScalarGridSpec(
        num_scalar_prefetch=2,
        grid=(n_blocks, f // tf),
        in_specs=[pl.BlockSpec((1, 1, rows), lambda i, j, be, bv: (i, 0, 0), memory_space=pltpu.SMEM),
                  pl.BlockSpec((1, 1, rows), lambda i, j, be, bv: (jnp.minimum(i + 1, n_blocks - 1), 0, 0),
                               memory_space=pltpu.SMEM),
                  pl.BlockSpec((1, 1, rows), lambda i, j, be, bv: (i, 0, 0), memory_space=pltpu.SMEM),
                  pl.BlockSpec((rows, 1), lambda i, j, be, bv: (i, 0)),
                  pl.BlockSpec(memory_space=pl.ANY),
                  pl.BlockSpec((None, None, d, tf), lambda i, j, be, bv: (layer, be[i], 0, j)),
                  pl.BlockSpec((None, None, d, tf), lambda i, j, be, bv: (layer, be[i], 0, j)),
                  pl.BlockSpec((None, None, tf, d), lambda i, j, be, bv: (layer, be[i], j, 0))],
        out_specs=pl.BlockSpec(memory_space=pl.ANY),
        scratch_shapes=[pltpu.VMEM((2, rows * SUBLANES, LANES), F32), pltpu.VMEM((rows, d), BF16),
                        pltpu.VMEM((rows, d), F32), pltpu.VMEM((rows * SUBLANES, LANES), F32),
                        pltpu.SemaphoreType.DMA((2,)), pltpu.SemaphoreType.DMA(())])
    return pl.pallas_call(
        kern,
        grid_spec=grid_spec,
        out_shape=jax.ShapeDtypeStruct(((TOP_K * n + rows) * SUBLANES, LANES), F32),
        compiler_params=_params("arbitrary", "arbitrary"),
        name="moe_ffn",
    )(block_e, block_valid, slot_tok, slot_tok, slot_dst, slot_gate, xn, wg, wu, wd)


def _combine_tiles(h_ref, y0_ref, y1_ref):
    tm, d = h_ref.shape
    pieces = d // LANES
    y0 = _from_token_tiles(y0_ref, tm, pieces)
    y1 = _from_token_tiles(y1_ref, tm, pieces)
    return jnp.concatenate([y0[s] + y1[s] for s in range(pieces)], axis=1) + h_ref[...]


def _combine_kernel(h_ref, y0_ref, y1_ref, o_ref):
    o_ref[...] = _combine_tiles(h_ref, y0_ref, y1_ref)


def _combine_norm_kernel(h_ref, y0_ref, y1_ref, g_ref, o_ref):
    o_ref[...] = _rms(_combine_tiles(h_ref, y0_ref, y1_ref), g_ref[...])


def moe_combine(h, y, final_gain=None, tm=512):
    n, d = h.shape
    tm = min(tm, n)
    nt = n // tm
    specs = [pl.BlockSpec((tm, d), lambda i: (i, 0)),
             pl.BlockSpec((tm * SUBLANES, LANES), lambda i: (i, 0)),
             pl.BlockSpec((tm * SUBLANES, LANES), lambda i: (nt + i, 0))]
    args = [h, y, y]
    kern = _combine_kernel
    if final_gain is not None:
        specs.append(pl.BlockSpec((1, d), lambda i: (0, 0)))
        args.append(final_gain.reshape(1, d))
        kern = _combine_norm_kernel
    return pl.pallas_call(
        kern,
        grid=(n // tm,),
        in_specs=specs,
        out_specs=pl.BlockSpec((tm, d), lambda i: (i, 0)),
        out_shape=jax.ShapeDtypeStruct((n, d), F32),
        compiler_params=_params("parallel"),
        name="moe_combine",
    )(*args)


def _final_norm_kernel(h_ref, g_ref, o_ref):
    o_ref[...] = _rms(h_ref[...], g_ref[...])


def final_norm(h, gain, tm=512):
    n, d = h.shape
    tm = min(tm, n)
    return pl.pallas_call(
        _final_norm_kernel,
        grid=(n // tm,),
        in_specs=[pl.BlockSpec((tm, d), lambda i: (i, 0)), pl.BlockSpec((1, d), lambda i: (0, 0))],
        out_specs=pl.BlockSpec((tm, d), lambda i: (i, 0)),
        out_shape=jax.ShapeDtypeStruct((n, d), F32),
        compiler_params=_params("parallel"),
        name="final_norm",
    )(h, gain.reshape(1, d))


def kernel(x, norm_mix, norm_ffn, norm_final, ret_w_in, ret_gn, ret_w_out, s5_w_in, s5_a_re, s5_a_im, s5_log_step, s5_b_re, s5_b_im, s5_c_re, s5_c_im, s5_d, s5_w_out, moba_w_in, moba_w_out, ffn_w_gate, ffn_w_up, ffn_w_down, moe_router, moe_w_gate, moe_w_up, moe_w_down):
    batch, seq, d = x.shape
    depth = norm_mix.shape[0]
    n = batch * seq
    h = x.reshape(n, d).astype(F32)
    i_ret = i_s5 = i_moba = i_dense = i_moe = 0
    out = None
    for i in range(depth):
        mixer = i % N_MIXERS
        if mixer == 0:
            proj = norm_matmul(h, norm_mix[i], ret_w_in, i_ret, BF16)
            y = retention_core(proj, ret_gn[i_ret], batch, seq)
            h = matmul_residual(y, ret_w_out, i_ret, h)
            i_ret += 1
        elif mixer == 1:
            sub = math.gcd(seq, S5_SUBCHUNK)
            u = norm_matmul(h, norm_mix[i], s5_w_in, i_s5, F32)
            ops = _s5_operators(s5_a_re[i_s5], s5_a_im[i_s5], s5_log_step[i_s5], s5_b_re[i_s5],
                                s5_b_im[i_s5], s5_c_re[i_s5], s5_c_im[i_s5], sub)
            y = s5_core(u, s5_d[i_s5], ops, batch, seq, sub)
            h = matmul_glu_residual(y, s5_w_out, i_s5, h)
            i_s5 += 1
        else:
            qkv = norm_matmul(h, norm_mix[i], moba_w_in, i_moba, BF16)
            o = moba_core(qkv, batch, seq)
            h = matmul_residual(o, moba_w_out, i_moba, h)
            i_moba += 1
        last = i == depth - 1
        if i % 2 == 0:
            h = dense_ffn(h, norm_ffn[i], ffn_w_gate, ffn_w_up, ffn_w_down, i_dense)
            i_dense += 1
            if last:
                out = final_norm(h, norm_final)
        else:
            xn, logits = router(h, norm_ffn[i], moe_router, i_moe)
            tables = _route_tables(logits, min(MOE_ROWS, n))
            y = moe_ffn(xn, tables, moe_w_gate, moe_w_up, moe_w_down, i_moe)
            i_moe += 1
            if last:
                out = moe_combine(h, y, norm_final)
            else:
                h = moe_combine(h, y)
    return out.reshape(batch, seq, d).astype(x.dtype)
```

```python
import functools
import math

import numpy as np
import jax
import jax.numpy as jnp
from jax import lax
from jax.experimental import pallas as pl
from jax.experimental.pallas import tpu as pltpu

F32 = jnp.float32
BF16 = jnp.bfloat16

NORM_EPS = 1e-6
NEG_INF = -1e30
N_MIXERS = 3

RET_HEADS = 4
RET_CHUNK = 256
S5_GROUP = 16
S5_LANE_GROUPS = 8
S5_SUBCHUNK = 16
S5_ROWS = 128
S5_DT_MIN = 1e-3
MOBA_HEADS = 8
MOBA_BLOCK = 256
MOBA_TOPK = 3
MOBA_GROUP = 4
MOBA_QUERY_BLOCKS = 4
TOP_K = 2
MOE_ROWS = 1024
MOE_ISSUE_UNROLL = 8

V7X_VMEM_LIMIT_BYTES = 56 * 1024 * 1024
LANES = 128
SUBLANES = 8


def _params(*sem):
    return pltpu.CompilerParams(dimension_semantics=sem, vmem_limit_bytes=V7X_VMEM_LIMIT_BYTES)


def _rms(x, gain):
    return x * lax.rsqrt(jnp.mean(x * x, axis=-1, keepdims=True) + NORM_EPS) * gain


def _dot(a, b):
    return jnp.dot(a, b, preferred_element_type=F32)


def _dot_nt(a, b):
    return lax.dot_general(a, b, (((1,), (1,)), ((), ())), preferred_element_type=F32)


def _dot_tn(a, b):
    return lax.dot_general(a, b, (((0,), (0,)), ((), ())), preferred_element_type=F32)


def _norm_matmul_kernel(*refs, combine):
    if combine:
        x_ref, y0_ref, y1_ref, g_ref, w_ref, o_ref, h_ref, xn_ref = refs
    else:
        x_ref, g_ref, w_ref, o_ref, xn_ref = refs

    @pl.when(pl.program_id(1) == 0)
    def _():
        if combine:
            x = _combine_tiles(x_ref, y0_ref, y1_ref)
            h_ref[...] = x
        else:
            x = x_ref[...]
        xn_ref[...] = _rms(x, g_ref[...]).astype(BF16)

    o_ref[...] = _dot(xn_ref[...], w_ref[...]).astype(o_ref.dtype)


def norm_matmul(x, gain, w, layer, out_dtype, routed=None, tm=1024, tn=1024):
    m, d = x.shape
    n = w.shape[2]
    combine = routed is not None
    tm, tn = min(tm // 2 if combine else tm, m), min(tn, n)
    nt = m // tm
    in_specs = [pl.BlockSpec((tm, d), lambda i, j: (i, 0))]
    args = [x]
    out_specs = [pl.BlockSpec((tm, tn), lambda i, j: (i, j))]
    out_shape = [jax.ShapeDtypeStruct((m, n), out_dtype)]
    if combine:
        in_specs += [pl.BlockSpec((tm * SUBLANES, LANES), lambda i, j: (i, 0)),
                     pl.BlockSpec((tm * SUBLANES, LANES), lambda i, j: (nt + i, 0))]
        args += [routed, routed]
        out_specs.append(pl.BlockSpec((tm, d), lambda i, j: (i, 0)))
        out_shape.append(jax.ShapeDtypeStruct((m, d), F32))
    in_specs += [pl.BlockSpec((1, d), lambda i, j: (0, 0)), pl.BlockSpec((d, tn), lambda i, j: (0, j))]
    args += [gain.reshape(1, d), w[layer].astype(BF16)]
    out = pl.pallas_call(
        functools.partial(_norm_matmul_kernel, combine=combine),
        grid=(nt, n // tn),
        in_specs=in_specs,
        out_specs=out_specs,
        out_shape=out_shape,
        scratch_shapes=[pltpu.VMEM((tm, d), BF16)],
        compiler_params=_params("parallel", "arbitrary"),
        name="norm_matmul",
    )(*args)
    return out if combine else out[0]


def _out_proj_kernel(*refs, n_w, route):
    a_ref, w_refs, r_ref = refs[0], refs[1:1 + n_w], refs[1 + n_w]
    rest = refs[2 + n_w:]
    if route:
        g_ref, rw_ref, o_ref, xn_ref, lg_ref = rest[:5]
        wb_refs = rest[5:]
    else:
        o_ref, wb_refs = rest[0], rest[1:]

    @pl.when(pl.program_id(0) == 0)
    def _():
        for w_ref, wb_ref in zip(w_refs, wb_refs):
            wb_ref[...] = w_ref[...].astype(BF16)

    a = a_ref[...].astype(BF16)
    z = _dot(a, wb_refs[0][...])
    if n_w == 2:
        z = z * jax.nn.sigmoid(_dot(a, wb_refs[1][...]))
    h = r_ref[...] + z
    o_ref[...] = h
    if route:
        xn = _rms(h, g_ref[...])
        _to_token_tiles(xn_ref, xn)
        lg_ref[...] = jnp.concatenate(
            [jnp.sum(xn * rw_ref[e:e + 1, :], axis=1, keepdims=True) for e in range(rw_ref.shape[0])], axis=1)


def out_projection(a, w, layer, res, glu=False, route=None, tm=512):
    m, k = a.shape
    n_w = 2 if glu else 1
    n = w.shape[2] // n_w
    tm = min(tm, m)
    in_specs = [pl.BlockSpec((tm, k), lambda i: (i, 0))]
    in_specs += [pl.BlockSpec((None, k, n), lambda i, c=c: (layer, 0, c)) for c in range(n_w)]
    in_specs.append(pl.BlockSpec((tm, n), lambda i: (i, 0)))
    args = [a] + [w] * n_w + [res]
    out_specs = [pl.BlockSpec((tm, n), lambda i: (i, 0))]
    out_shape = [jax.ShapeDtypeStruct((m, n), F32)]
    if route:
        gain, w_router, r_layer = route
        e = w_router.shape[2]
        assert n == SUBLANES * LANES
        in_specs += [pl.BlockSpec((1, n), lambda i: (0, 0)), pl.BlockSpec((e, n), lambda i: (0, 0))]
        args += [gain.reshape(1, n), w_router[r_layer].astype(F32).T]
        out_specs += [pl.BlockSpec((tm * SUBLANES, LANES), lambda i: (i, 0)), pl.BlockSpec((tm, e), lambda i: (i, 0))]
        out_shape += [jax.ShapeDtypeStruct((m * SUBLANES, LANES), F32), jax.ShapeDtypeStruct((m, e), F32)]
    out = pl.pallas_call(
        functools.partial(_out_proj_kernel, n_w=n_w, route=bool(route)),
        grid=(m // tm,),
        in_specs=in_specs,
        out_specs=out_specs,
        out_shape=out_shape,
        scratch_shapes=[pltpu.VMEM((k, n), BF16)] * n_w,
        compiler_params=_params("arbitrary"),
        name="out_projection",
    )(*args)
    return out if route else out[0]


def _swiglu_step(xn, wg_ref, wu_ref, wd_ref):
    g = _dot(xn, wg_ref[...].astype(BF16))
    u = _dot(xn, wu_ref[...].astype(BF16))
    a = (g * jax.nn.sigmoid(g) * u).astype(BF16)
    return _dot(a, wd_ref[...].astype(BF16))


def _ffn_kernel(x_ref, g_ref, wg_ref, wu_ref, wd_ref, o_ref, xn_ref, acc_ref):
    j = pl.program_id(1)

    @pl.when(j == 0)
    def _():
        xn_ref[...] = _rms(x_ref[...], g_ref[...]).astype(BF16)
        acc_ref[...] = jnp.zeros_like(acc_ref)

    acc_ref[...] += _swiglu_step(xn_ref[...], wg_ref, wu_ref, wd_ref)

    @pl.when(j == pl.num_programs(1) - 1)
    def _():
        o_ref[...] = x_ref[...] + acc_ref[...]


def dense_ffn(x, gain, wg, wu, wd, layer, tm=1024, tf=512):
    m, d = x.shape
    f = wg.shape[2]
    tm, tf = min(tm, m), min(tf, f)
    return pl.pallas_call(
        _ffn_kernel,
        grid=(m // tm, f // tf),
        in_specs=[pl.BlockSpec((tm, d), lambda i, j: (i, 0)),
                  pl.BlockSpec((1, d), lambda i, j: (0, 0)),
                  pl.BlockSpec((None, d, tf), lambda i, j: (layer, 0, j)),
                  pl.BlockSpec((None, d, tf), lambda i, j: (layer, 0, j)),
                  pl.BlockSpec((None, tf, d), lambda i, j: (layer, j, 0))],
        out_specs=pl.BlockSpec((tm, d), lambda i, j: (i, 0)),
        out_shape=jax.ShapeDtypeStruct((m, d), F32),
        scratch_shapes=[pltpu.VMEM((tm, d), BF16), pltpu.VMEM((tm, d), F32)],
        compiler_params=_params("parallel", "arbitrary"),
        name="dense_ffn",
    )(x, gain.reshape(1, d), wg, wu, wd)


def _retention_kernel(q_ref, k_ref, v_ref, g_ref, dec_ref, qd_ref, kd_ref, gn_ref, o_ref, r_ref,
                      *, heads, dk, dv, chunk_decay):
    @pl.when(pl.program_id(1) == 0)
    def _():
        r_ref[...] = jnp.zeros_like(r_ref)

    for h in range(heads):
        qh = q_ref[:, h * dk:(h + 1) * dk]
        kh = k_ref[:, h * dk:(h + 1) * dk]
        vh = v_ref[:, h * dv:(h + 1) * dv]
        s = _dot_nt(qh, kh) * dec_ref[h]
        y = _dot(s.astype(BF16), vh)
        y = y + _dot(qh, r_ref[h].astype(BF16)) * qd_ref[h]
        kd = (kh.astype(F32) * kd_ref[h]).astype(BF16)
        r_ref[h] = r_ref[h] * chunk_decay[h] + _dot_tn(kd, vh)
        mu = jnp.mean(y, axis=-1, keepdims=True)
        yc = y - mu
        var = jnp.mean(yc * yc, axis=-1, keepdims=True)
        yn = yc * lax.rsqrt(var + NORM_EPS) * gn_ref[:, h * dv:(h + 1) * dv]
        gate = g_ref[:, h * dv:(h + 1) * dv].astype(F32)
        o_ref[:, h * dv:(h + 1) * dv] = (yn * (gate * jax.nn.sigmoid(gate))).astype(o_ref.dtype)


def retention_core(proj, gn_gain, batch, seq):
    d = proj.shape[1] // 6
    heads = RET_HEADS
    dk, dv = d // heads, 2 * d // heads
    c = math.gcd(seq, RET_CHUNK)
    nc = seq // c
    log_gamma = np.log1p(-np.exp2(-5.0 - np.arange(heads, dtype=np.float64)))
    idx = np.arange(c, dtype=np.float64)
    diff = idx[:, None] - idx[None, :]
    scale = dk ** -0.5
    decay = np.where(diff >= 0, np.exp(log_gamma[:, None, None] * np.maximum(diff, 0.0)), 0.0) * scale
    q_decay = np.exp(log_gamma[:, None] * (idx + 1.0))[:, :, None]
    k_decay = np.exp(log_gamma[:, None] * (c - 1.0 - idx))[:, :, None] * scale
    chunk_decay = tuple(float(v) for v in np.exp(log_gamma * c))
    kern = functools.partial(_retention_kernel, heads=heads, dk=dk, dv=dv, chunk_decay=chunk_decay)
    row = lambda b, n: b * nc + n
    return pl.pallas_call(
        kern,
        grid=(batch, nc),
        in_specs=[pl.BlockSpec((c, d), lambda b, n: (row(b, n), 0)),
                  pl.BlockSpec((c, d), lambda b, n: (row(b, n), 1)),
                  pl.BlockSpec((c, 2 * d), lambda b, n: (row(b, n), 1)),
                  pl.BlockSpec((c, 2 * d), lambda b, n: (row(b, n), 2)),
                  pl.BlockSpec((heads, c, c), lambda b, n: (0, 0, 0)),
                  pl.BlockSpec((heads, c, 1), lambda b, n: (0, 0, 0)),
                  pl.BlockSpec((heads, c, 1), lambda b, n: (0, 0, 0)),
                  pl.BlockSpec((1, 2 * d), lambda b, n: (0, 0))],
        out_specs=pl.BlockSpec((c, 2 * d), lambda b, n: (row(b, n), 0)),
        out_shape=jax.ShapeDtypeStruct((batch * seq, 2 * d), BF16),
        scratch_shapes=[pltpu.VMEM((heads, dk, dv), F32)],
        compiler_params=_params("parallel", "arbitrary"),
        name="retention_core",
    )(proj, proj, proj, proj, jnp.asarray(decay, F32), jnp.asarray(q_decay, F32),
      jnp.asarray(k_decay, F32), gn_gain.reshape(1, 2 * d).astype(F32))


def _gelu_tanh(x):
    return 0.5 * x * (1.0 + jnp.tanh(math.sqrt(2.0 / math.pi) * (x + 0.044715 * (x * x * x))))


def _s5_expand(tc_ref, sc_ref, qc_ref, es_ref, eq_ref, m1_ref, m2_ref, *, sub, ga, half):
    width = sub * LANES
    n_ch = LANES // ga
    p = half // ga

    def keep_own_group(x, row_group, col_group):
        return jnp.where(row_group == col_group, x, 0.0).astype(BF16)

    lag = _dot(tc_ref[0], eq_ref[...])
    ra = lax.broadcasted_iota(jnp.int32, lag.shape, 0) // n_ch
    cb = (lax.broadcasted_iota(jnp.int32, lag.shape, 1) % LANES) // n_ch
    lag = keep_own_group(lag, ra, cb)
    for s in range(sub):
        rs = slice(s * LANES, (s + 1) * LANES)
        if s:
            m1_ref[rs, 0:s * LANES] = jnp.zeros((LANES, s * LANES), BF16)
        m1_ref[rs, s * LANES:width] = lag[:, 0:(sub - s) * LANES]
        sx = _dot(sc_ref[0, rs, :], es_ref[...])
        ra = lax.broadcasted_iota(jnp.int32, sx.shape, 0) // n_ch
        cb = (lax.broadcasted_iota(jnp.int32, sx.shape, 1) % half) // p
        m1_ref[rs, width:width + 2 * half] = keep_own_group(sx, ra, cb)
    for r in range(2 * ga):
        qx = _dot(qc_ref[0, r * p:(r + 1) * p, :], eq_ref[...])
        cb = (lax.broadcasted_iota(jnp.int32, qx.shape, 1) % LANES) // n_ch
        m2_ref[r * p:(r + 1) * p, :] = keep_own_group(qx, r % ga, cb)


def _s5_kernel(u_ref, d_ref, tc_ref, sc_ref, qc_ref, es_ref, eq_ref, al_ref, o_ref,
               m1_ref, m2_ref, xl_ref, xp_ref, st_ref, *, sub, rows, steps_per_seq, ga):
    half = st_ref.shape[1] // 2
    width = sub * LANES

    @pl.when(pl.program_id(1) == 0)
    def _():
        _s5_expand(tc_ref, sc_ref, qc_ref, es_ref, eq_ref, m1_ref, m2_ref, sub=sub, ga=ga, half=half)

    @pl.when(pl.program_id(1) % steps_per_seq == 0)
    def _():
        st_ref[...] = jnp.zeros_like(st_ref)

    us = [u_ref[pl.ds(t, rows, stride=sub), :] for t in range(sub)]
    ucat = jnp.concatenate([u.astype(BF16) for u in us], axis=1)
    y_all = _dot(ucat, m1_ref[...])
    xl_ref[...] = y_all[:, width:]

    a_re = al_ref[0, 0:1, :]
    a_im = al_ref[0, 1:2, :]

    def step(c, carry):
        xr, xi = carry
        xp_ref[pl.ds(c, 1), 0:half] = xr
        xp_ref[pl.ds(c, 1), half:2 * half] = xi
        lr = xl_ref[pl.ds(c, 1), 0:half]
        li = xl_ref[pl.ds(c, 1), half:2 * half]
        return a_re * xr - a_im * xi + lr, a_re * xi + a_im * xr + li

    xr, xi = lax.fori_loop(0, rows, step, (st_ref[0:1, 0:half], st_ref[0:1, half:2 * half]))
    st_ref[0:1, 0:half] = xr
    st_ref[0:1, half:2 * half] = xi

    y_cross = _dot(xp_ref[...].astype(BF16), m2_ref[...])
    skip = d_ref[...]
    for t in range(sub):
        y = y_all[:, t * LANES:(t + 1) * LANES] + y_cross[:, t * LANES:(t + 1) * LANES] + skip * us[t]
        o_ref[pl.ds(t, rows, stride=sub), :] = _gelu_tanh(y)


def _s5_operators(a_re, a_im, log_step, b_re, b_im, c_re, c_im, sub):
    g, p = a_re.shape
    n = b_re.shape[2]
    ga = S5_LANE_GROUPS
    j = g // ga
    dt = jnp.exp(log_step.astype(F32))[:, None]
    ar, ai = a_re.astype(F32), a_im.astype(F32)
    mag = jnp.exp(ar * dt)
    abar_re = mag * jnp.cos(ai * dt)
    abar_im = mag * jnp.sin(ai * dt)
    den = ar * ar + ai * ai
    nr, ni = abar_re - 1.0, abar_im
    f_re = (nr * ar + ni * ai) / den
    f_im = (ni * ar - nr * ai) / den
    br, bi = b_re.astype(F32), b_im.astype(F32)
    bb_re = f_re[..., None] * br - f_im[..., None] * bi
    bb_im = f_re[..., None] * bi + f_im[..., None] * br
    cr, ci = c_re.astype(F32), c_im.astype(F32)
    tau = jnp.arange(sub + 1, dtype=F32)[:, None, None]
    pw_mag = jnp.exp(tau * (ar * dt)[None])
    pw_re = pw_mag * jnp.cos(tau * (ai * dt)[None])
    pw_im = pw_mag * jnp.sin(tau * (ai * dt)[None])
    hp = lax.Precision.HIGHEST
    ab_re = pw_re[..., None] * bb_re[None] - pw_im[..., None] * bb_im[None]
    ab_im = pw_re[..., None] * bb_im[None] + pw_im[..., None] * bb_re[None]
    lag = (jnp.einsum('tgpm,gnp->tgmn', ab_re[:sub], cr, precision=hp)
           - jnp.einsum('tgpm,gnp->tgmn', ab_im[:sub], ci, precision=hp))
    tc = lag.reshape(sub, j, ga, n, n).transpose(1, 2, 3, 0, 4).reshape(j, ga * n, sub * n)
    rev = sub - 1 - jnp.arange(sub)
    sc = jnp.stack([ab_re[rev], ab_im[rev]], axis=0).reshape(2, sub, j, ga, p, n)
    sc = sc.transpose(2, 1, 3, 5, 0, 4).reshape(j, sub * ga * n, 2 * p)
    q_re = cr[None] * pw_re[1:][:, :, None, :] - ci[None] * pw_im[1:][:, :, None, :]
    q_im = -(cr[None] * pw_im[1:][:, :, None, :] + ci[None] * pw_re[1:][:, :, None, :])
    qc = jnp.stack([q_re, q_im], axis=0).reshape(2, sub, j, ga, n, p)
    qc = qc.transpose(2, 0, 3, 5, 1, 4).reshape(j, 2 * ga * p, sub * n)
    al = jnp.stack([pw_re[sub].reshape(j, ga * p), pw_im[sub].reshape(j, ga * p)], axis=1)
    return tc.astype(BF16), sc.astype(BF16), qc.astype(BF16), al


def _s5_spread_matrices(sub, ga, n, p):
    half = ga * p
    r = np.arange(2 * p)
    c = np.arange(2 * half)
    es = (r[:, None] // p == c[None, :] // half) & (r[:, None] % p == c[None, :] % p)
    r = np.arange(sub * n)
    c = np.arange(sub * ga * n)
    eq = (r[:, None] // n == c[None, :] // (ga * n)) & (r[:, None] % n == c[None, :] % n)
    return jnp.asarray(es, BF16), jnp.asarray(eq, BF16)


def s5_core(u, d_skip, ops, batch, seq, sub):
    tc, sc, qc, al = ops
    n_tok, w = u.shape
    j = tc.shape[0]
    half = al.shape[2]
    ga = S5_LANE_GROUPS
    es, eq = _s5_spread_matrices(sub, ga, LANES // ga, half // ga)
    rows = min(S5_ROWS, seq // sub)
    step_tok = rows * sub
    steps_per_seq = seq // step_tok
    width = sub * LANES
    kern = functools.partial(_s5_kernel, sub=sub, rows=rows, steps_per_seq=steps_per_seq, ga=ga)
    tile = lambda a: pl.BlockSpec((1,) + a.shape[1:], lambda jj, i: (jj, 0, 0))
    whole = lambda a: pl.BlockSpec(a.shape, lambda jj, i: (0, 0))
    return pl.pallas_call(
        kern,
        grid=(j, n_tok // step_tok),
        in_specs=[pl.BlockSpec((step_tok, LANES), lambda jj, i: (i, jj)),
                  pl.BlockSpec((1, LANES), lambda jj, i: (0, jj)),
                  tile(tc), tile(sc), tile(qc), whole(es), whole(eq), tile(al)],
        out_specs=pl.BlockSpec((step_tok, LANES), lambda jj, i: (i, jj)),
        out_shape=jax.ShapeDtypeStruct((n_tok, w), F32),
        scratch_shapes=[pltpu.VMEM((width, width + 2 * half), BF16), pltpu.VMEM((2 * half, width), BF16),
                        pltpu.VMEM((rows, 2 * half), F32), pltpu.VMEM((rows, 2 * half), F32),
                        pltpu.VMEM((8, 2 * half), F32)],
        compiler_params=_params("parallel", "arbitrary"),
        name="s5_core",
    )(u, d_skip.reshape(1, w).astype(F32), tc, sc, qc, es, eq, al)


MOBA_AUX_CONSTS = 6


def _moba_kernel(cst_ref, q_ref, k_ref, v_ref, o_ref, km_ref, ka_ref, vt_ref, sa_ref, sb_ref, acc_ref,
                 *, blk, topk, scale, group, qblocks):
    h = pl.program_id(1)
    i = pl.program_id(2)
    first = i * qblocks
    nb = k_ref.shape[0] // blk
    hd = q_ref.shape[1]
    aux = ka_ref.shape[2] - hd
    nc = MOBA_AUX_CONSTS

    @pl.when(i == 0)
    def _():
        km_ref[...] = jnp.zeros_like(km_ref)
        jl = lax.broadcasted_iota(jnp.int32, (blk, aux), 0)
        lane = lax.broadcasted_iota(jnp.int32, (blk, aux), 1)

        def fill(n, c):
            st = pl.multiple_of(n * blk, blk)
            kb = k_ref[pl.ds(st, blk), :]
            km_ref[pl.ds(n + nc, 1), :] = jnp.mean(kb.astype(F32), axis=0, keepdims=True)
            pat = jnp.where(lane < nc // 2, n, jnp.where(lane < nc, jl, (lane - nc == n).astype(jnp.int32)))
            ka_ref[n, :, 0:hd] = kb
            ka_ref[n, :, hd:hd + aux] = pat.astype(F32).astype(BF16)
            vt_ref[n] = v_ref[pl.ds(st, blk), :].astype(F32).T.astype(BF16)
            return c

        lax.fori_loop(0, nb, fill, 0)

    q = q_ref[...]
    gate = lax.dot_general(q.astype(F32), km_ref[...], (((1,), (1,)), ((), ())),
                           preferred_element_type=F32, precision=lax.Precision.HIGHEST)
    lane = lax.broadcasted_iota(jnp.int32, gate.shape, 1)
    own = first + lax.broadcasted_iota(jnp.int32, gate.shape, 0) // blk
    gate = jnp.where(jnp.logical_and(lane >= nc, lane - nc < own), gate, NEG_INF)
    chosen = jnp.zeros(gate.shape, jnp.bool_)
    lane_f = lane.astype(F32)
    for _ in range(topk):
        mx = jnp.max(gate, axis=1, keepdims=True)
        idx = jnp.min(jnp.where(gate == mx, lane_f, float(2 ** 20)), axis=1, keepdims=True)
        hit = lane_f == idx
        chosen = jnp.logical_or(chosen, jnp.logical_and(hit, mx > 0.5 * NEG_INF))
        gate = jnp.where(hit, NEG_INF, gate)

    qx = jnp.where(chosen, 0.0, NEG_INF)
    for c in range(nc):
        qx = jnp.where(lane == c, cst_ref[h, c], qx)
    qa = jnp.concatenate([q, qx.astype(BF16)], axis=1)
    a_full = cst_ref[h, 0] + cst_ref[h, 1] + cst_ref[h, 2]
    c_full = cst_ref[h, 3] + cst_ref[h, 4] + cst_ref[h, 5]

    jk = lax.broadcasted_iota(jnp.int32, (blk, blk), 0)
    jq = lax.broadcasted_iota(jnp.int32, (blk, blk), 1)
    c2 = scale * math.log2(math.e)
    m0, l0, acc0 = [], [], []
    for w in range(qblocks):
        bw = first + w
        st = _dot_nt(ka_ref[bw, :, 0:hd], q[w * blk:(w + 1) * blk])
        st = st + (c_full * jk.astype(F32) + a_full * bw.astype(F32))
        st = jnp.where(jk <= jq, st, NEG_INF)
        mw = jnp.max(st, axis=0, keepdims=True)
        p = jnp.exp2((st - mw) * c2)
        m0.append(mw)
        l0.append(jnp.sum(p, axis=0, keepdims=True))
        acc0.append(_dot(vt_ref[bw], p.astype(BF16)))
    m0, l0, acc0 = (jnp.concatenate(x, axis=1) for x in (m0, l0, acc0))

    n_groups = nb // group

    half = group // 2

    def scores(g, s_ref):
        tops = []
        for u in range(2):
            ka = ka_ref[pl.ds(g * group + u * half, half)].reshape(half * blk, hd + aux)
            st = _dot_nt(ka, qa)
            s_ref[u * half * blk:(u + 1) * half * blk, :] = st
            tops.append(jnp.max(st, axis=0, keepdims=True))
        return jnp.maximum(tops[0], tops[1])

    def softmax(st, top, m, l):
        m_new = jnp.maximum(m, top)
        alpha = jnp.exp2((m - m_new) * c2)
        p = jnp.exp2((st - m_new) * c2)
        return m_new, alpha * l + jnp.sum(p, axis=0, keepdims=True), alpha, p.astype(BF16)

    def weighted_values(g, pb):
        pv = _dot(vt_ref[g * group], pb[0:blk])
        for u in range(1, group):
            pv = pv + _dot(vt_ref[g * group + u], pb[u * blk:(u + 1) * blk])
        return pv

    top0 = scores(0, sa_ref)
    acc_ref[...] = acc0
    past = first + qblocks - 1

    def body(t, carry):
        m, l, top_a = carry
        g0 = 2 * t
        g1 = g0 + 1
        g2 = jnp.minimum(g0 + 2, n_groups - 1)
        top_b = scores(g1, sb_ref)
        m, l, alpha, pa = softmax(sa_ref[...], top_a, m, l)
        acc_ref[...] = alpha * acc_ref[...] + weighted_values(g0, pa)
        top_a = scores(g2, sa_ref)
        m, l, alpha, pb = softmax(sb_ref[...], top_b, m, l)
        acc_ref[...] = alpha * acc_ref[...] + weighted_values(g1, pb)
        return m, l, top_a

    m, l, _ = lax.fori_loop(0, (past + 2 * group - 1) // (2 * group), body, (m0, l0, top0))
    o_ref[...] = (acc_ref[...] / l).T.astype(o_ref.dtype)


def moba_core(qkv, batch, seq):
    d = qkv.shape[1] // 3
    heads = MOBA_HEADS
    hd = d // heads
    blk = MOBA_BLOCK
    assert seq % blk == 0
    nb = seq // blk
    topk = max(1, min(MOBA_TOPK, nb - 1))
    group = math.gcd(nb, MOBA_GROUP)
    aux = LANES
    assert MOBA_AUX_CONSTS + nb <= aux and (nb // group) % 2 == 0
    scale = hd ** -0.5
    slopes = np.exp2(-8.0 * (np.arange(heads, dtype=np.float64) + 1.0) / heads)
    consts = []
    for val in (slopes * blk / scale, slopes / scale):
        rest = jnp.asarray(val, F32)
        for _ in range(MOBA_AUX_CONSTS // 2):
            piece = rest.astype(BF16).astype(F32)
            consts.append(piece)
            rest = rest - piece
    cst = jnp.stack(consts, axis=1)
    qblocks = math.gcd(nb, MOBA_QUERY_BLOCKS)
    nq = nb // qblocks
    kern = functools.partial(_moba_kernel, blk=blk, topk=topk, scale=scale, group=group, qblocks=qblocks)
    return pl.pallas_call(
        kern,
        grid=(batch, heads, nq),
        in_specs=[pl.BlockSpec(memory_space=pltpu.SMEM),
                  pl.BlockSpec((qblocks * blk, hd), lambda b, h, i: (b * nq + i, h)),
                  pl.BlockSpec((seq, hd), lambda b, h, i: (b, heads + h)),
                  pl.BlockSpec((seq, hd), lambda b, h, i: (b, 2 * heads + h))],
        out_specs=pl.BlockSpec((qblocks * blk, hd), lambda b, h, i: (b * nq + i, h)),
        out_shape=jax.ShapeDtypeStruct((batch * seq, d), BF16),
        scratch_shapes=[pltpu.VMEM((aux, hd), F32), pltpu.VMEM((nb, blk, hd + aux), BF16),
                        pltpu.VMEM((nb, hd, blk), BF16),
                        pltpu.VMEM((group * blk, qblocks * blk), F32),
                        pltpu.VMEM((group * blk, qblocks * blk), F32),
                        pltpu.VMEM((hd, qblocks * blk), F32)],
        compiler_params=_params("parallel", "parallel", "arbitrary"),
        name="moba_core",
    )(cst, qkv, qkv, qkv)


def _to_token_tiles(ref, x):
    pieces = x.shape[1] // LANES
    for s in range(pieces):
        ref[pl.ds(s, x.shape[0], stride=pieces), :] = x[:, s * LANES:(s + 1) * LANES]


def _from_token_tiles(ref, tokens, pieces):
    return [ref[pl.ds(s, tokens, stride=pieces), :] for s in range(pieces)]


def _route_tables(logits, rows):
    n, e = logits.shape
    top_logits, top_idx = lax.top_k(logits, TOP_K)
    gates = jax.nn.softmax(top_logits, axis=-1).reshape(-1)
    e_flat = top_idx.reshape(-1).astype(jnp.int32)
    order = jnp.argsort(e_flat).astype(jnp.int32)
    counts = jnp.sum((e_flat[:, None] == jnp.arange(e, dtype=jnp.int32)[None, :]).astype(jnp.int32), axis=0)
    padded = ((counts + rows - 1) // rows) * rows
    start = jnp.cumsum(counts) - counts
    pend = jnp.cumsum(padded)
    pstart = pend - padded
    n_blocks = -(-(n * TOP_K) // rows) + e
    slot = jnp.arange(n_blocks * rows, dtype=jnp.int32)
    e_s = jnp.minimum(jnp.searchsorted(pend, slot, side='right'), e - 1).astype(jnp.int32)
    rank = slot - pstart[e_s]
    valid = rank < counts[e_s]
    pair = order[jnp.clip(start[e_s] + rank, 0, n * TOP_K - 1)]
    tok = pair // TOP_K
    slot_tok = jnp.where(valid, tok, 0).astype(jnp.int32)
    slot_dst = jnp.where(valid, (pair % TOP_K) * n + tok, TOP_K * n + slot % rows).astype(jnp.int32)
    slot_gate = jnp.where(valid, gates[pair], 0.0).astype(F32)
    block_e = e_s[::rows]
    block_valid = (jnp.arange(n_blocks, dtype=jnp.int32) * rows < pend[-1]).astype(jnp.int32)
    return (block_e, block_valid, slot_tok.reshape(n_blocks, 1, rows),
            slot_dst.reshape(n_blocks, 1, rows), slot_gate.reshape(n_blocks * rows, 1))


def _moe_kernel(be_ref, bv_ref, tok_ref, tok_next_ref, dst_ref, gate_ref, xn_hbm, wg_ref, wu_ref, wd_ref, y_hbm,
                xg_ref, xb_ref, acc_ref, yb_ref, sem_in, sem_out, *, rows):
    i = pl.program_id(0)
    j = pl.program_id(1)
    last_j = pl.num_programs(1) - 1
    valid = bv_ref[i] != 0
    nxt = jnp.minimum(i + 1, pl.num_programs(0) - 1)
    next_valid = jnp.logical_and(i + 1 < pl.num_programs(0), bv_ref[nxt] != 0)
    pieces = SUBLANES

    def tile(ref, t):
        return ref.at[pl.ds(pl.multiple_of(t * pieces, pieces), pieces)]

    def gather(table_ref, slot):
        def issue(r, c):
            pltpu.make_async_copy(tile(xn_hbm, table_ref[0, 0, r]), tile(xg_ref.at[slot], r), sem_in.at[slot]).start()
            return c

        lax.fori_loop(0, rows, issue, 0, unroll=MOE_ISSUE_UNROLL)

    def gather_wait(slot):
        pltpu.make_async_copy(xg_ref.at[slot], xg_ref.at[slot], sem_in.at[slot]).wait()

    def scatter():
        def issue(r, c):
            pltpu.make_async_copy(tile(yb_ref, r), tile(y_hbm, dst_ref[0, 0, r]), sem_out).start()
            return c

        lax.fori_loop(0, rows, issue, 0, unroll=MOE_ISSUE_UNROLL)

    def scatter_wait():
        pltpu.make_async_copy(yb_ref, yb_ref, sem_out).wait()

    @pl.when(jnp.logical_and(i == 0, j == 0))
    def _():
        yb_ref[...] = jnp.zeros_like(yb_ref)
        n_spare = rows * pieces
        spare = pltpu.make_async_copy(yb_ref, y_hbm.at[pl.ds(y_hbm.shape[0] - n_spare, n_spare)], sem_out)
        spare.start()
        spare.wait()

        @pl.when(valid)
        def _():
            gather(tok_ref, 0)

    @pl.when(jnp.logical_and(valid, j == 0))
    def _():
        for slot in range(2):
            @pl.when(i % 2 == slot)
            def _():
                gather_wait(slot)
                for s, piece in enumerate(_from_token_tiles(xg_ref.at[slot], rows, pieces)):
                    xb_ref[:, s * LANES:(s + 1) * LANES] = piece.astype(BF16)

                @pl.when(next_valid)
                def _():
                    gather(tok_next_ref, 1 - slot)

        acc_ref[...] = jnp.zeros_like(acc_ref)

    @pl.when(valid)
    def _():
        acc_ref[...] += _swiglu_step(xb_ref[...], wg_ref, wu_ref, wd_ref)

    @pl.when(jnp.logical_and(valid, j == last_j))
    def _():
        @pl.when(i > 0)
        def _():
            scatter_wait()

        _to_token_tiles(yb_ref, acc_ref[...] * gate_ref[...])
        scatter()

        @pl.when(jnp.logical_not(next_valid))
        def _():
            scatter_wait()


def moe_ffn(xn, tables, wg, wu, wd, layer, tf=512):
    d, f = wg.shape[2], wg.shape[3]
    n = xn.shape[0] // SUBLANES
    block_e, block_valid, slot_tok, slot_dst, slot_gate = tables
    n_blocks, _, rows = slot_tok.shape
    tf = min(tf, f)
    kern = functools.partial(_moe_kernel, rows=rows)
    grid_spec = pltpu.PrefetchScalarGridSpec(
        num_scalar_prefetch=2,
        grid=(n_blocks, f // tf),
        in_specs=[pl.BlockSpec((1, 1, rows), lambda i, j, be, bv: (i, 0, 0), memory_space=pltpu.SMEM),
                  pl.BlockSpec((1, 1, rows), lambda i, j, be, bv: (jnp.minimum(i + 1, n_blocks - 1), 0, 0),
                               memory_space=pltpu.SMEM),
                  pl.BlockSpec((1, 1, rows), lambda i, j, be, bv: (i, 0, 0), memory_space=pltpu.SMEM),
                  pl.BlockSpec((rows, 1), lambda i, j, be, bv: (i, 0)),
                  pl.BlockSpec(memory_space=pl.ANY),
                  pl.BlockSpec((None, None, d, tf), lambda i, j, be, bv: (layer, be[i], 0, j)),
                  pl.BlockSpec((None, None, d, tf), lambda i, j, be, bv: (layer, be[i], 0, j)),
                  pl.BlockSpec((None, None, tf, d), lambda i, j, be, bv: (layer, be[i], j, 0))],
        out_specs=pl.BlockSpec(memory_space=pl.ANY),
        scratch_shapes=[pltpu.VMEM((2, rows * SUBLANES, LANES), F32), pltpu.VMEM((rows, d), BF16),
                        pltpu.VMEM((rows, d), F32), pltpu.VMEM((rows * SUBLANES, LANES), F32),
                        pltpu.SemaphoreType.DMA((2,)), pltpu.SemaphoreType.DMA(())])
    return pl.pallas_call(
        kern,
        grid_spec=grid_spec,
        out_shape=jax.ShapeDtypeStruct(((TOP_K * n + rows) * SUBLANES, LANES), F32),
        compiler_params=_params("arbitrary", "arbitrary"),
        name="moe_ffn",
    )(block_e, block_valid, slot_tok, slot_tok, slot_dst, slot_gate, xn, wg, wu, wd)


def _combine_tiles(h_ref, y0_ref, y1_ref):
    tm, d = h_ref.shape
    pieces = d // LANES
    y0 = _from_token_tiles(y0_ref, tm, pieces)
    y1 = _from_token_tiles(y1_ref, tm, pieces)
    return jnp.concatenate([y0[s] + y1[s] for s in range(pieces)], axis=1) + h_ref[...]


def _combine_norm_kernel(h_ref, y0_ref, y1_ref, g_ref, o_ref):
    o_ref[...] = _rms(_combine_tiles(h_ref, y0_ref, y1_ref), g_ref[...])


def moe_combine_norm(h, routed, gain, tm=512):
    n, d = h.shape
    tm = min(tm, n)
    nt = n // tm
    return pl.pallas_call(
        _combine_norm_kernel,
        grid=(nt,),
        in_specs=[pl.BlockSpec((tm, d), lambda i: (i, 0)),
                  pl.BlockSpec((tm * SUBLANES, LANES), lambda i: (i, 0)),
                  pl.BlockSpec((tm * SUBLANES, LANES), lambda i: (nt + i, 0)),
                  pl.BlockSpec((1, d), lambda i: (0, 0))],
        out_specs=pl.BlockSpec((tm, d), lambda i: (i, 0)),
        out_shape=jax.ShapeDtypeStruct((n, d), F32),
        compiler_params=_params("parallel"),
        name="moe_combine_norm",
    )(h, routed, routed, gain.reshape(1, d))


def _final_norm_kernel(h_ref, g_ref, o_ref):
    o_ref[...] = _rms(h_ref[...], g_ref[...])


def final_norm(h, gain, tm=512):
    n, d = h.shape
    tm = min(tm, n)
    return pl.pallas_call(
        _final_norm_kernel,
        grid=(n // tm,),
        in_specs=[pl.BlockSpec((tm, d), lambda i: (i, 0)), pl.BlockSpec((1, d), lambda i: (0, 0))],
        out_specs=pl.BlockSpec((tm, d), lambda i: (i, 0)),
        out_shape=jax.ShapeDtypeStruct((n, d), F32),
        compiler_params=_params("parallel"),
        name="final_norm",
    )(h, gain.reshape(1, d))


def kernel(x, norm_mix, norm_ffn, norm_final, ret_w_in, ret_gn, ret_w_out, s5_w_in, s5_a_re, s5_a_im, s5_log_step, s5_b_re, s5_b_im, s5_c_re, s5_c_im, s5_d, s5_w_out, moba_w_in, moba_w_out, ffn_w_gate, ffn_w_up, ffn_w_down, moe_router, moe_w_gate, moe_w_up, moe_w_down):
    batch, seq, d = x.shape
    depth = norm_mix.shape[0]
    n = batch * seq
    h = x.reshape(n, d).astype(F32)
    i_ret = i_s5 = i_moba = i_dense = i_moe = 0
    out = None
    routed = None
    for i in range(depth):
        mixer = i % N_MIXERS
        moe = i % 2 == 1
        route = (norm_ffn[i], moe_router, i_moe) if moe else None

        def mixer_input(w_in, idx, dtype):
            res = norm_matmul(h, norm_mix[i], w_in, idx, dtype, routed=routed)
            return res if routed is not None else (res, h)

        if mixer == 0:
            proj, h = mixer_input(ret_w_in, i_ret, BF16)
            y = retention_core(proj, ret_gn[i_ret], batch, seq)
            res = out_projection(y, ret_w_out, i_ret, h, route=route)
            i_ret += 1
        elif mixer == 1:
            sub = math.gcd(seq, S5_SUBCHUNK)
            u, h = mixer_input(s5_w_in, i_s5, F32)
            ops = _s5_operators(s5_a_re[i_s5], s5_a_im[i_s5], s5_log_step[i_s5], s5_b_re[i_s5],
                                s5_b_im[i_s5], s5_c_re[i_s5], s5_c_im[i_s5], sub)
            y = s5_core(u, s5_d[i_s5], ops, batch, seq, sub)
            res = out_projection(y, s5_w_out, i_s5, h, glu=True, route=route)
            i_s5 += 1
        else:
            qkv, h = mixer_input(moba_w_in, i_moba, BF16)
            o = moba_core(qkv, batch, seq)
            res = out_projection(o, moba_w_out, i_moba, h, route=route)
            i_moba += 1
        routed = None
        last = i == depth - 1
        if not moe:
            h = dense_ffn(res, norm_ffn[i], ffn_w_gate, ffn_w_up, ffn_w_down, i_dense)
            i_dense += 1
            if last:
                out = final_norm(h, norm_final)
        else:
            h, xn, logits = res
            tables = _route_tables(logits, min(MOE_ROWS, n))
            routed = moe_ffn(xn, tables, moe_w_gate, moe_w_up, moe_w_down, i_moe)
            i_moe += 1
            if last:
                out = moe_combine_norm(h, routed, norm_final)
    return out.reshape(batch, seq, d).astype(x.dtype)
```

```python
import functools
import math

import numpy as np
import jax
import jax.numpy as jnp
from jax import lax
from jax.experimental import pallas as pl
from jax.experimental.pallas import tpu as pltpu

F32 = jnp.float32
BF16 = jnp.bfloat16

NORM_EPS = 1e-6
NEG_INF = -1e30
N_MIXERS = 3

RET_HEADS = 4
RET_CHUNK = 256
S5_GROUP = 16
S5_LANE_GROUPS = 8
S5_SUBCHUNK = 16
S5_ROWS = 128
S5_DT_MIN = 1e-3
MOBA_HEADS = 8
MOBA_BLOCK = 256
MOBA_TOPK = 3
MOBA_GROUP = 4
MOBA_QUERY_BLOCKS = 4
TOP_K = 2
MOE_ROWS = 1024
MOE_ISSUE_UNROLL = 8

V7X_VMEM_LIMIT_BYTES = 56 * 1024 * 1024
LANES = 128
SUBLANES = 8


def _params(*sem):
    return pltpu.CompilerParams(dimension_semantics=sem, vmem_limit_bytes=V7X_VMEM_LIMIT_BYTES)


def _rms(x, gain):
    return x * lax.rsqrt(jnp.mean(x * x, axis=-1, keepdims=True) + NORM_EPS) * gain


def _dot(a, b):
    return jnp.dot(a, b, preferred_element_type=F32)


def _dot_nt(a, b):
    return lax.dot_general(a, b, (((1,), (1,)), ((), ())), preferred_element_type=F32)


def _dot_tn(a, b):
    return lax.dot_general(a, b, (((0,), (0,)), ((), ())), preferred_element_type=F32)


def _norm_matmul_kernel(x_ref, g_ref, w_ref, o_ref, xn_ref):
    @pl.when(pl.program_id(1) == 0)
    def _():
        xn_ref[...] = _rms(x_ref[...], g_ref[...]).astype(BF16)

    o_ref[...] = _dot(xn_ref[...], w_ref[...]).astype(o_ref.dtype)


def norm_matmul(x, gain, w, layer, out_dtype, tm=1024, tn=1024):
    m, d = x.shape
    n = w.shape[2]
    tm, tn = min(tm, m), min(tn, n)
    return pl.pallas_call(
        _norm_matmul_kernel,
        grid=(m // tm, n // tn),
        in_specs=[pl.BlockSpec((tm, d), lambda i, j: (i, 0)),
                  pl.BlockSpec((1, d), lambda i, j: (0, 0)),
                  pl.BlockSpec((d, tn), lambda i, j: (0, j))],
        out_specs=pl.BlockSpec((tm, tn), lambda i, j: (i, j)),
        out_shape=jax.ShapeDtypeStruct((m, n), out_dtype),
        scratch_shapes=[pltpu.VMEM((tm, d), BF16)],
        compiler_params=_params("parallel", "arbitrary"),
        name="norm_matmul",
    )(x, gain.reshape(1, d), w[layer].astype(BF16))


def _out_proj_kernel(*refs, n_w, route):
    a_ref, w_refs, r_ref = refs[0], refs[1:1 + n_w], refs[1 + n_w]
    rest = refs[2 + n_w:]
    if route:
        g_ref, rw_ref, o_ref, xn_ref, lg_ref = rest[:5]
        wb_refs = rest[5:]
    else:
        o_ref, wb_refs = rest[0], rest[1:]

    @pl.when(pl.program_id(0) == 0)
    def _():
        for w_ref, wb_ref in zip(w_refs, wb_refs):
            wb_ref[...] = w_ref[...].astype(BF16)

    a = a_ref[...].astype(BF16)
    z = _dot(a, wb_refs[0][...])
    if n_w == 2:
        z = z * jax.nn.sigmoid(_dot(a, wb_refs[1][...]))
    h = r_ref[...] + z
    o_ref[...] = h
    if route:
        xn = _rms(h, g_ref[...])
        _to_token_tiles(xn_ref, xn)
        lg_ref[...] = jnp.concatenate(
            [jnp.sum(xn * rw_ref[e:e + 1, :], axis=1, keepdims=True) for e in range(rw_ref.shape[0])], axis=1)


def out_projection(a, w, layer, res, glu=False, route=None, tm=512):
    m, k = a.shape
    n_w = 2 if glu else 1
    n = w.shape[2] // n_w
    tm = min(tm, m)
    in_specs = [pl.BlockSpec((tm, k), lambda i: (i, 0))]
    in_specs += [pl.BlockSpec((None, k, n), lambda i, c=c: (layer, 0, c)) for c in range(n_w)]
    in_specs.append(pl.BlockSpec((tm, n), lambda i: (i, 0)))
    args = [a] + [w] * n_w + [res]
    out_specs = [pl.BlockSpec((tm, n), lambda i: (i, 0))]
    out_shape = [jax.ShapeDtypeStruct((m, n), F32)]
    if route:
        gain, w_router, r_layer = route
        e = w_router.shape[2]
        assert n == SUBLANES * LANES
        in_specs += [pl.BlockSpec((1, n), lambda i: (0, 0)), pl.BlockSpec((e, n), lambda i: (0, 0))]
        args += [gain.reshape(1, n), w_router[r_layer].astype(F32).T]
        out_specs += [pl.BlockSpec((tm * SUBLANES, LANES), lambda i: (i, 0)), pl.BlockSpec((tm, e), lambda i: (i, 0))]
        out_shape += [jax.ShapeDtypeStruct((m * SUBLANES, LANES), F32), jax.ShapeDtypeStruct((m, e), F32)]
    out = pl.pallas_call(
        functools.partial(_out_proj_kernel, n_w=n_w, route=bool(route)),
        grid=(m // tm,),
        in_specs=in_specs,
        out_specs=out_specs,
        out_shape=out_shape,
        scratch_shapes=[pltpu.VMEM((k, n), BF16)] * n_w,
        compiler_params=_params("arbitrary"),
        name="out_projection",
    )(*args)
    return out if route else out[0]


def _swiglu_step(xn, wg_ref, wu_ref, wd_ref):
    g = _dot(xn, wg_ref[...].astype(BF16))
    u = _dot(xn, wu_ref[...].astype(BF16))
    a = (g * jax.nn.sigmoid(g) * u).astype(BF16)
    return _dot(a, wd_ref[...].astype(BF16))


def _ffn_kernel(x_ref, g_ref, wg_ref, wu_ref, wd_ref, o_ref, xn_ref, acc_ref):
    j = pl.program_id(1)

    @pl.when(j == 0)
    def _():
        xn_ref[...] = _rms(x_ref[...], g_ref[...]).astype(BF16)
        acc_ref[...] = jnp.zeros_like(acc_ref)

    acc_ref[...] += _swiglu_step(xn_ref[...], wg_ref, wu_ref, wd_ref)

    @pl.when(j == pl.num_programs(1) - 1)
    def _():
        o_ref[...] = x_ref[...] + acc_ref[...]


def dense_ffn(x, gain, wg, wu, wd, layer, tm=1024, tf=512):
    m, d = x.shape
    f = wg.shape[2]
    tm, tf = min(tm, m), min(tf, f)
    return pl.pallas_call(
        _ffn_kernel,
        grid=(m // tm, f // tf),
        in_specs=[pl.BlockSpec((tm, d), lambda i, j: (i, 0)),
                  pl.BlockSpec((1, d), lambda i, j: (0, 0)),
                  pl.BlockSpec((None, d, tf), lambda i, j: (layer, 0, j)),
                  pl.BlockSpec((None, d, tf), lambda i, j: (layer, 0, j)),
                  pl.BlockSpec((None, tf, d), lambda i, j: (layer, j, 0))],
        out_specs=pl.BlockSpec((tm, d), lambda i, j: (i, 0)),
        out_shape=jax.ShapeDtypeStruct((m, d), F32),
        scratch_shapes=[pltpu.VMEM((tm, d), BF16), pltpu.VMEM((tm, d), F32)],
        compiler_params=_params("parallel", "arbitrary"),
        name="dense_ffn",
    )(x, gain.reshape(1, d), wg, wu, wd)


def _retention_kernel(q_ref, k_ref, v_ref, g_ref, dec_ref, qd_ref, kd_ref, gn_ref, o_ref, r_ref,
                      *, heads, dk, dv, chunk_decay):
    @pl.when(pl.program_id(1) == 0)
    def _():
        r_ref[...] = jnp.zeros_like(r_ref)

    for h in range(heads):
        qh = q_ref[:, h * dk:(h + 1) * dk]
        kh = k_ref[:, h * dk:(h + 1) * dk]
        vh = v_ref[:, h * dv:(h + 1) * dv]
        s = _dot_nt(qh, kh) * dec_ref[h]
        y = _dot(s.astype(BF16), vh)
        y = y + _dot(qh, r_ref[h].astype(BF16)) * qd_ref[h]
        kd = (kh.astype(F32) * kd_ref[h]).astype(BF16)
        r_ref[h] = r_ref[h] * chunk_decay[h] + _dot_tn(kd, vh)
        mu = jnp.mean(y, axis=-1, keepdims=True)
        yc = y - mu
        var = jnp.mean(yc * yc, axis=-1, keepdims=True)
        yn = yc * lax.rsqrt(var + NORM_EPS) * gn_ref[:, h * dv:(h + 1) * dv]
        gate = g_ref[:, h * dv:(h + 1) * dv].astype(F32)
        o_ref[:, h * dv:(h + 1) * dv] = (yn * (gate * jax.nn.sigmoid(gate))).astype(o_ref.dtype)


def retention_core(proj, gn_gain, batch, seq):
    d = proj.shape[1] // 6
    heads = RET_HEADS
    dk, dv = d // heads, 2 * d // heads
    c = math.gcd(seq, RET_CHUNK)
    nc = seq // c
    log_gamma = np.log1p(-np.exp2(-5.0 - np.arange(heads, dtype=np.float64)))
    idx = np.arange(c, dtype=np.float64)
    diff = idx[:, None] - idx[None, :]
    scale = dk ** -0.5
    decay = np.where(diff >= 0, np.exp(log_gamma[:, None, None] * np.maximum(diff, 0.0)), 0.0) * scale
    q_decay = np.exp(log_gamma[:, None] * (idx + 1.0))[:, :, None]
    k_decay = np.exp(log_gamma[:, None] * (c - 1.0 - idx))[:, :, None] * scale
    chunk_decay = tuple(float(v) for v in np.exp(log_gamma * c))
    kern = functools.partial(_retention_kernel, heads=heads, dk=dk, dv=dv, chunk_decay=chunk_decay)
    row = lambda b, n: b * nc + n
    return pl.pallas_call(
        kern,
        grid=(batch, nc),
        in_specs=[pl.BlockSpec((c, d), lambda b, n: (row(b, n), 0)),
                  pl.BlockSpec((c, d), lambda b, n: (row(b, n), 1)),
                  pl.BlockSpec((c, 2 * d), lambda b, n: (row(b, n), 1)),
                  pl.BlockSpec((c, 2 * d), lambda b, n: (row(b, n), 2)),
                  pl.BlockSpec((heads, c, c), lambda b, n: (0, 0, 0)),
                  pl.BlockSpec((heads, c, 1), lambda b, n: (0, 0, 0)),
                  pl.BlockSpec((heads, c, 1), lambda b, n: (0, 0, 0)),
                  pl.BlockSpec((1, 2 * d), lambda b, n: (0, 0))],
        out_specs=pl.BlockSpec((c, 2 * d), lambda b, n: (row(b, n), 0)),
        out_shape=jax.ShapeDtypeStruct((batch * seq, 2 * d), BF16),
        scratch_shapes=[pltpu.VMEM((heads, dk, dv), F32)],
        compiler_params=_params("parallel", "arbitrary"),
        name="retention_core",
    )(proj, proj, proj, proj, jnp.asarray(decay, F32), jnp.asarray(q_decay, F32),
      jnp.asarray(k_decay, F32), gn_gain.reshape(1, 2 * d).astype(F32))


def _gelu_tanh(x):
    return 0.5 * x * (1.0 + jnp.tanh(math.sqrt(2.0 / math.pi) * (x + 0.044715 * (x * x * x))))


def _s5_expand(tc_ref, sc_ref, qc_ref, es_ref, eq_ref, m1_ref, m2_ref, *, sub, ga, half):
    width = sub * LANES
    n_ch = LANES // ga
    p = half // ga

    def keep_own_group(x, row_group, col_group):
        return jnp.where(row_group == col_group, x, 0.0).astype(BF16)

    lag = _dot(tc_ref[0], eq_ref[...])
    ra = lax.broadcasted_iota(jnp.int32, lag.shape, 0) // n_ch
    cb = (lax.broadcasted_iota(jnp.int32, lag.shape, 1) % LANES) // n_ch
    lag = keep_own_group(lag, ra, cb)
    for s in range(sub):
        rs = slice(s * LANES, (s + 1) * LANES)
        if s:
            m1_ref[rs, 0:s * LANES] = jnp.zeros((LANES, s * LANES), BF16)
        m1_ref[rs, s * LANES:width] = lag[:, 0:(sub - s) * LANES]
        sx = _dot(sc_ref[0, rs, :], es_ref[...])
        ra = lax.broadcasted_iota(jnp.int32, sx.shape, 0) // n_ch
        cb = (lax.broadcasted_iota(jnp.int32, sx.shape, 1) % half) // p
        m1_ref[rs, width:width + 2 * half] = keep_own_group(sx, ra, cb)
    for r in range(2 * ga):
        qx = _dot(qc_ref[0, r * p:(r + 1) * p, :], eq_ref[...])
        cb = (lax.broadcasted_iota(jnp.int32, qx.shape, 1) % LANES) // n_ch
        m2_ref[r * p:(r + 1) * p, :] = keep_own_group(qx, r % ga, cb)


def _s5_kernel(u_ref, d_ref, tc_ref, sc_ref, qc_ref, es_ref, eq_ref, al_ref, o_ref,
               m1_ref, m2_ref, xl_ref, xp_ref, st_ref, *, sub, rows, steps_per_seq, ga):
    half = st_ref.shape[1] // 2
    width = sub * LANES

    @pl.when(pl.program_id(1) == 0)
    def _():
        _s5_expand(tc_ref, sc_ref, qc_ref, es_ref, eq_ref, m1_ref, m2_ref, sub=sub, ga=ga, half=half)

    @pl.when(pl.program_id(1) % steps_per_seq == 0)
    def _():
        st_ref[...] = jnp.zeros_like(st_ref)

    us = [u_ref[pl.ds(t, rows, stride=sub), :] for t in range(sub)]
    ucat = jnp.concatenate([u.astype(BF16) for u in us], axis=1)
    y_all = _dot(ucat, m1_ref[...])
    xl_ref[...] = y_all[:, width:]

    a_re = al_ref[0, 0:1, :]
    a_im = al_ref[0, 1:2, :]

    def step(c, carry):
        xr, xi = carry
        xp_ref[pl.ds(c, 1), 0:half] = xr
        xp_ref[pl.ds(c, 1), half:2 * half] = xi
        lr = xl_ref[pl.ds(c, 1), 0:half]
        li = xl_ref[pl.ds(c, 1), half:2 * half]
        return a_re * xr - a_im * xi + lr, a_re * xi + a_im * xr + li

    xr, xi = lax.fori_loop(0, rows, step, (st_ref[0:1, 0:half], st_ref[0:1, half:2 * half]))
    st_ref[0:1, 0:half] = xr
    st_ref[0:1, half:2 * half] = xi

    y_cross = _dot(xp_ref[...].astype(BF16), m2_ref[...])
    skip = d_ref[...]
    for t in range(sub):
        y = y_all[:, t * LANES:(t + 1) * LANES] + y_cross[:, t * LANES:(t + 1) * LANES] + skip * us[t]
        o_ref[pl.ds(t, rows, stride=sub), :] = _gelu_tanh(y)


def _s5_operators(a_re, a_im, log_step, b_re, b_im, c_re, c_im, sub):
    g, p = a_re.shape
    n = b_re.shape[2]
    ga = S5_LANE_GROUPS
    j = g // ga
    dt = jnp.exp(log_step.astype(F32))[:, None]
    ar, ai = a_re.astype(F32), a_im.astype(F32)
    mag = jnp.exp(ar * dt)
    abar_re = mag * jnp.cos(ai * dt)
    abar_im = mag * jnp.sin(ai * dt)
    den = ar * ar + ai * ai
    nr, ni = abar_re - 1.0, abar_im
    f_re = (nr * ar + ni * ai) / den
    f_im = (ni * ar - nr * ai) / den
    br, bi = b_re.astype(F32), b_im.astype(F32)
    bb_re = f_re[..., None] * br - f_im[..., None] * bi
    bb_im = f_re[..., None] * bi + f_im[..., None] * br
    cr, ci = c_re.astype(F32), c_im.astype(F32)
    tau = jnp.arange(sub + 1, dtype=F32)[:, None, None]
    pw_mag = jnp.exp(tau * (ar * dt)[None])
    pw_re = pw_mag * jnp.cos(tau * (ai * dt)[None])
    pw_im = pw_mag * jnp.sin(tau * (ai * dt)[None])
    hp = lax.Precision.HIGHEST
    ab_re = pw_re[..., None] * bb_re[None] - pw_im[..., None] * bb_im[None]
    ab_im = pw_re[..., None] * bb_im[None] + pw_im[..., None] * bb_re[None]
    lag = (jnp.einsum('tgpm,gnp->tgmn', ab_re[:sub], cr, precision=hp)
           - jnp.einsum('tgpm,gnp->tgmn', ab_im[:sub], ci, precision=hp))
    tc = lag.reshape(sub, j, ga, n, n).transpose(1, 2, 3, 0, 4).reshape(j, ga * n, sub * n)
    rev = sub - 1 - jnp.arange(sub)
    sc = jnp.stack([ab_re[rev], ab_im[rev]], axis=0).reshape(2, sub, j, ga, p, n)
    sc = sc.transpose(2, 1, 3, 5, 0, 4).reshape(j, sub * ga * n, 2 * p)
    q_re = cr[None] * pw_re[1:][:, :, None, :] - ci[None] * pw_im[1:][:, :, None, :]
    q_im = -(cr[None] * pw_im[1:][:, :, None, :] + ci[None] * pw_re[1:][:, :, None, :])
    qc = jnp.stack([q_re, q_im], axis=0).reshape(2, sub, j, ga, n, p)
    qc = qc.transpose(2, 0, 3, 5, 1, 4).reshape(j, 2 * ga * p, sub * n)
    al = jnp.stack([pw_re[sub].reshape(j, ga * p), pw_im[sub].reshape(j, ga * p)], axis=1)
    return tc.astype(BF16), sc.astype(BF16), qc.astype(BF16), al


def _s5_spread_matrices(sub, ga, n, p):
    half = ga * p
    r = np.arange(2 * p)
    c = np.arange(2 * half)
    es = (r[:, None] // p == c[None, :] // half) & (r[:, None] % p == c[None, :] % p)
    r = np.arange(sub * n)
    c = np.arange(sub * ga * n)
    eq = (r[:, None] // n == c[None, :] // (ga * n)) & (r[:, None] % n == c[None, :] % n)
    return jnp.asarray(es, BF16), jnp.asarray(eq, BF16)


def s5_core(u, d_skip, ops, batch, seq, sub):
    tc, sc, qc, al = ops
    n_tok, w = u.shape
    j = tc.shape[0]
    half = al.shape[2]
    ga = S5_LANE_GROUPS
    es, eq = _s5_spread_matrices(sub, ga, LANES // ga, half // ga)
    rows = min(S5_ROWS, seq // sub)
    step_tok = rows * sub
    steps_per_seq = seq // step_tok
    width = sub * LANES
    kern = functools.partial(_s5_kernel, sub=sub, rows=rows, steps_per_seq=steps_per_seq, ga=ga)
    tile = lambda a: pl.BlockSpec((1,) + a.shape[1:], lambda jj, i: (jj, 0, 0))
    whole = lambda a: pl.BlockSpec(a.shape, lambda jj, i: (0, 0))
    return pl.pallas_call(
        kern,
        grid=(j, n_tok // step_tok),
        in_specs=[pl.BlockSpec((step_tok, LANES), lambda jj, i: (i, jj)),
                  pl.BlockSpec((1, LANES), lambda jj, i: (0, jj)),
                  tile(tc), tile(sc), tile(qc), whole(es), whole(eq), tile(al)],
        out_specs=pl.BlockSpec((step_tok, LANES), lambda jj, i: (i, jj)),
        out_shape=jax.ShapeDtypeStruct((n_tok, w), F32),
        scratch_shapes=[pltpu.VMEM((width, width + 2 * half), BF16), pltpu.VMEM((2 * half, width), BF16),
                        pltpu.VMEM((rows, 2 * half), F32), pltpu.VMEM((rows, 2 * half), F32),
                        pltpu.VMEM((8, 2 * half), F32)],
        compiler_params=_params("parallel", "arbitrary"),
        name="s5_core",
    )(u, d_skip.reshape(1, w).astype(F32), tc, sc, qc, es, eq, al)


MOBA_AUX_CONSTS = 6


def _moba_kernel(cst_ref, q_ref, k_ref, v_ref, o_ref, km_ref, ka_ref, vt_ref, sa_ref, sb_ref, acc_ref,
                 *, blk, topk, scale, group, qblocks):
    h = pl.program_id(1)
    i = pl.program_id(2)
    first = i * qblocks
    nb = k_ref.shape[0] // blk
    hd = q_ref.shape[1]
    aux = ka_ref.shape[2] - hd
    nc = MOBA_AUX_CONSTS

    @pl.when(i == 0)
    def _():
        km_ref[...] = jnp.zeros_like(km_ref)
        jl = lax.broadcasted_iota(jnp.int32, (blk, aux), 0)
        lane = lax.broadcasted_iota(jnp.int32, (blk, aux), 1)

        def fill(n, c):
            st = pl.multiple_of(n * blk, blk)
            kb = k_ref[pl.ds(st, blk), :]
            km_ref[pl.ds(n + nc, 1), :] = jnp.mean(kb.astype(F32), axis=0, keepdims=True)
            pat = jnp.where(lane < nc // 2, n, jnp.where(lane < nc, jl, (lane - nc == n).astype(jnp.int32)))
            ka_ref[n, :, 0:hd] = kb
            ka_ref[n, :, hd:hd + aux] = pat.astype(F32).astype(BF16)
            vt_ref[n] = v_ref[pl.ds(st, blk), :].astype(F32).T.astype(BF16)
            return c

        lax.fori_loop(0, nb, fill, 0)

    q = q_ref[...]
    rest = km_ref[...]
    gate = None
    for _ in range(3):
        piece = rest.astype(BF16)
        rest = rest - piece.astype(F32)
        part = _dot_nt(q, piece)
        gate = part if gate is None else gate + part
    lane = lax.broadcasted_iota(jnp.int32, gate.shape, 1)
    own = first + lax.broadcasted_iota(jnp.int32, gate.shape, 0) // blk
    gate = jnp.where(jnp.logical_and(lane >= nc, lane - nc < own), gate, NEG_INF)
    chosen = jnp.zeros(gate.shape, jnp.bool_)
    lane_f = lane.astype(F32)
    for _ in range(topk):
        mx = jnp.max(gate, axis=1, keepdims=True)
        idx = jnp.min(jnp.where(gate == mx, lane_f, float(2 ** 20)), axis=1, keepdims=True)
        hit = lane_f == idx
        chosen = jnp.logical_or(chosen, jnp.logical_and(hit, mx > 0.5 * NEG_INF))
        gate = jnp.where(hit, NEG_INF, gate)

    qx = jnp.where(chosen, 0.0, NEG_INF)
    for c in range(nc):
        qx = jnp.where(lane == c, cst_ref[h, c], qx)
    qa = jnp.concatenate([q, qx.astype(BF16)], axis=1)
    a_full = cst_ref[h, 0] + cst_ref[h, 1] + cst_ref[h, 2]
    c_full = cst_ref[h, 3] + cst_ref[h, 4] + cst_ref[h, 5]

    jk = lax.broadcasted_iota(jnp.int32, (blk, blk), 0)
    jq = lax.broadcasted_iota(jnp.int32, (blk, blk), 1)
    c2 = scale * math.log2(math.e)
    m0, l0, acc0 = [], [], []
    for w in range(qblocks):
        bw = first + w
        st = _dot_nt(ka_ref[bw, :, 0:hd], q[w * blk:(w + 1) * blk])
        st = st + (c_full * jk.astype(F32) + a_full * bw.astype(F32))
        st = jnp.where(jk <= jq, st, NEG_INF)
        mw = jnp.max(st, axis=0, keepdims=True)
        p = jnp.exp2((st - mw) * c2)
        m0.append(mw)
        l0.append(jnp.sum(p, axis=0, keepdims=True))
        acc0.append(_dot(vt_ref[bw], p.astype(BF16)))
    m0, l0, acc0 = (jnp.concatenate(x, axis=1) for x in (m0, l0, acc0))

    n_groups = nb // group

    half = group // 2

    def scores(g, s_ref):
        tops = []
        for u in range(2):
            ka = ka_ref[pl.ds(g * group + u * half, half)].reshape(half * blk, hd + aux)
            st = _dot_nt(ka, qa)
            s_ref[u * half * blk:(u + 1) * half * blk, :] = st
            tops.append(jnp.max(st, axis=0, keepdims=True))
        return jnp.maximum(tops[0], tops[1])

    def softmax(st, top, m, l):
        m_new = jnp.maximum(m, top)
        alpha = jnp.exp2((m - m_new) * c2)
        p = jnp.exp2((st - m_new) * c2)
        return m_new, alpha * l + jnp.sum(p, axis=0, keepdims=True), alpha, p.astype(BF16)

    def weighted_values(g, pb):
        pv = _dot(vt_ref[g * group], pb[0:blk])
        for u in range(1, group):
            pv = pv + _dot(vt_ref[g * group + u], pb[u * blk:(u + 1) * blk])
        return pv

    top0 = scores(0, sa_ref)
    acc_ref[...] = acc0
    past = first + qblocks - 1

    def body(t, carry):
        m, l, top_a = carry
        g0 = 2 * t
        g1 = g0 + 1
        g2 = jnp.minimum(g0 + 2, n_groups - 1)
        top_b = scores(g1, sb_ref)
        m, l, alpha, pa = softmax(sa_ref[...], top_a, m, l)
        acc_ref[...] = alpha * acc_ref[...] + weighted_values(g0, pa)
        top_a = scores(g2, sa_ref)
        m, l, alpha, pb = softmax(sb_ref[...], top_b, m, l)
        acc_ref[...] = alpha * acc_ref[...] + weighted_values(g1, pb)
        return m, l, top_a

    m, l, _ = lax.fori_loop(0, (past + 2 * group - 1) // (2 * group), body, (m0, l0, top0))
    o_ref[...] = (acc_ref[...] / l).T.astype(o_ref.dtype)


def moba_core(qkv, batch, seq):
    d = qkv.shape[1] // 3
    heads = MOBA_HEADS
    hd = d // heads
    blk = MOBA_BLOCK
    assert seq % blk == 0
    nb = seq // blk
    topk = max(1, min(MOBA_TOPK, nb - 1))
    group = math.gcd(nb, MOBA_GROUP)
    aux = LANES
    assert MOBA_AUX_CONSTS + nb <= aux and (nb // group) % 2 == 0
    scale = hd ** -0.5
    slopes = np.exp2(-8.0 * (np.arange(heads, dtype=np.float64) + 1.0) / heads)
    consts = []
    for val in (slopes * blk / scale, slopes / scale):
        rest = jnp.asarray(val, F32)
        for _ in range(MOBA_AUX_CONSTS // 2):
            piece = rest.astype(BF16).astype(F32)
            consts.append(piece)
            rest = rest - piece
    cst = jnp.stack(consts, axis=1)
    qblocks = math.gcd(nb, MOBA_QUERY_BLOCKS)
    nq = nb // qblocks
    kern = functools.partial(_moba_kernel, blk=blk, topk=topk, scale=scale, group=group, qblocks=qblocks)
    return pl.pallas_call(
        kern,
        grid=(batch, heads, nq),
        in_specs=[pl.BlockSpec(memory_space=pltpu.SMEM),
                  pl.BlockSpec((qblocks * blk, hd), lambda b, h, i: (b * nq + i, h)),
                  pl.BlockSpec((seq, hd), lambda b, h, i: (b, heads + h)),
                  pl.BlockSpec((seq, hd), lambda b, h, i: (b, 2 * heads + h))],
        out_specs=pl.BlockSpec((qblocks * blk, hd), lambda b, h, i: (b * nq + i, h)),
        out_shape=jax.ShapeDtypeStruct((batch * seq, d), BF16),
        scratch_shapes=[pltpu.VMEM((aux, hd), F32), pltpu.VMEM((nb, blk, hd + aux), BF16),
                        pltpu.VMEM((nb, hd, blk), BF16),
                        pltpu.VMEM((group * blk, qblocks * blk), F32),
                        pltpu.VMEM((group * blk, qblocks * blk), F32),
                        pltpu.VMEM((hd, qblocks * blk), F32)],
        compiler_params=_params("parallel", "parallel", "arbitrary"),
        name="moba_core",
    )(cst, qkv, qkv, qkv)


def _to_token_tiles(ref, x):
    pieces = x.shape[1] // LANES
    for s in range(pieces):
        ref[pl.ds(s, x.shape[0], stride=pieces), :] = x[:, s * LANES:(s + 1) * LANES]


def _from_token_tiles(ref, tokens, pieces):
    return [ref[pl.ds(s, tokens, stride=pieces), :] for s in range(pieces)]


def _route_tables(logits, rows):
    n, e = logits.shape
    n_pairs = n * TOP_K
    i32 = jnp.int32
    top_logits, top_idx = lax.top_k(logits, TOP_K)
    gates = jax.nn.softmax(top_logits, axis=-1).reshape(-1)
    e_flat = top_idx.reshape(-1).astype(i32)
    e_sorted, pair_sorted, gate_sorted = lax.sort((e_flat, jnp.arange(n_pairs, dtype=i32), gates),
                                                  num_keys=1, is_stable=True)
    experts = jnp.arange(e, dtype=i32)
    start = jnp.searchsorted(e_sorted, experts, side='left', method='compare_all').astype(i32)
    counts = jnp.searchsorted(e_sorted, experts, side='right', method='compare_all').astype(i32) - start
    padded = ((counts + rows - 1) // rows) * rows
    pend = jnp.cumsum(padded)
    pstart = pend - padded
    n_blocks = -(-n_pairs // rows) + e
    blk0 = jnp.arange(n_blocks, dtype=i32) * rows
    block_e = jnp.minimum(jnp.searchsorted(pend, blk0, side='right', method='compare_all'), e - 1).astype(i32)
    first = blk0 - pstart[block_e]
    live = jnp.clip(counts[block_e] - first, 0, rows)
    off = jnp.clip(start[block_e] + first, 0, n_pairs)

    def windows(a):
        a = jnp.concatenate([a, jnp.zeros((rows,), a.dtype)])
        return jax.vmap(lambda o: lax.dynamic_slice(a, (o,), (rows,)))(off)

    pair_w, gate_w = windows(pair_sorted), windows(gate_sorted)
    r = jnp.arange(rows, dtype=i32)[None, :]
    valid = r < live[:, None]
    tok = pair_w // TOP_K
    slot_tok = jnp.where(valid, tok, 0).astype(i32)
    slot_dst = jnp.where(valid, (pair_w % TOP_K) * n + tok, TOP_K * n + r).astype(i32)
    slot_gate = jnp.where(valid, gate_w, 0.0).astype(F32)
    block_valid = (blk0 < pend[-1]).astype(i32)
    return (block_e, block_valid, slot_tok.reshape(n_blocks, 1, rows),
            slot_dst.reshape(n_blocks, 1, rows), slot_gate.reshape(n_blocks * rows, 1))


def _moe_kernel(be_ref, bv_ref, tok_ref, tok_next_ref, dst_ref, gate_ref, xn_hbm, wg_ref, wu_ref, wd_ref, y_hbm,
                xg_ref, xb_ref, acc_ref, yb_ref, sem_in, sem_out, *, rows):
    i = pl.program_id(0)
    j = pl.program_id(1)
    last_j = pl.num_programs(1) - 1
    valid = bv_ref[i] != 0
    nxt = jnp.minimum(i + 1, pl.num_programs(0) - 1)
    next_valid = jnp.logical_and(i + 1 < pl.num_programs(0), bv_ref[nxt] != 0)
    pieces = SUBLANES

    def tile(ref, t):
        return ref.at[pl.ds(pl.multiple_of(t * pieces, pieces), pieces)]

    def gather(table_ref, slot):
        def issue(r, c):
            pltpu.make_async_copy(tile(xn_hbm, table_ref[0, 0, r]), tile(xg_ref.at[slot], r), sem_in.at[slot]).start()
            return c

        lax.fori_loop(0, rows, issue, 0, unroll=MOE_ISSUE_UNROLL)

    def gather_wait(slot):
        pltpu.make_async_copy(xg_ref.at[slot], xg_ref.at[slot], sem_in.at[slot]).wait()

    def scatter():
        def issue(r, c):
            pltpu.make_async_copy(tile(yb_ref, r), tile(y_hbm, dst_ref[0, 0, r]), sem_out).start()
            return c

        lax.fori_loop(0, rows, issue, 0, unroll=MOE_ISSUE_UNROLL)

    def scatter_wait():
        pltpu.make_async_copy(yb_ref, yb_ref, sem_out).wait()

    @pl.when(jnp.logical_and(i == 0, j == 0))
    def _():
        yb_ref[...] = jnp.zeros_like(yb_ref)
        n_spare = rows * pieces
        spare = pltpu.make_async_copy(yb_ref, y_hbm.at[pl.ds(y_hbm.shape[0] - n_spare, n_spare)], sem_out)
        spare.start()
        spare.wait()

        @pl.when(valid)
        def _():
            gather(tok_ref, 0)

    @pl.when(jnp.logical_and(valid, j == 0))
    def _():
        for slot in range(2):
            @pl.when(i % 2 == slot)
            def _():
                gather_wait(slot)
                for s, piece in enumerate(_from_token_tiles(xg_ref.at[slot], rows, pieces)):
                    xb_ref[:, s * LANES:(s + 1) * LANES] = piece.astype(BF16)

                @pl.when(next_valid)
                def _():
                    gather(tok_next_ref, 1 - slot)

        acc_ref[...] = jnp.zeros_like(acc_ref)

    @pl.when(valid)
    def _():
        acc_ref[...] += _swiglu_step(xb_ref[...], wg_ref, wu_ref, wd_ref)

    @pl.when(jnp.logical_and(valid, j == last_j))
    def _():
        @pl.when(i > 0)
        def _():
            scatter_wait()

        _to_token_tiles(yb_ref, acc_ref[...] * gate_ref[...])
        scatter()

        @pl.when(jnp.logical_not(next_valid))
        def _():
            scatter_wait()


def moe_ffn(xn, tables, wg, wu, wd, layer, tf=512):
    d, f = wg.shape[2], wg.shape[3]
    n = xn.shape[0] // SUBLANES
    block_e, block_valid, slot_tok, slot_dst, slot_gate = tables
    n_blocks, _, rows = slot_tok.shape
    tf = min(tf, f)
    kern = functools.partial(_moe_kernel, rows=rows)
    grid_spec = pltpu.PrefetchScalarGridSpec(
        num_scalar_prefetch=2,
        grid=(n_blocks, f // tf),
        in_specs=[pl.BlockSpec((1, 1, rows), lambda i, j, be, bv: (i, 0, 0), memory_space=pltpu.SMEM),
                  pl.BlockSpec((1, 1, rows), lambda i, j, be, bv: (jnp.minimum(i + 1, n_blocks - 1), 0, 0),
                               memory_space=pltpu.SMEM),
                  pl.BlockSpec((1, 1, rows), lambda i, j, be, bv: (i, 0, 0), memory_space=pltpu.SMEM),
                  pl.BlockSpec((rows, 1), lambda i, j, be, bv: (i, 0)),
                  pl.BlockSpec(memory_space=pl.ANY),
                  pl.BlockSpec((None, None, d, tf), lambda i, j, be, bv: (layer, be[i], 0, j)),
                  pl.BlockSpec((None, None, d, tf), lambda i, j, be, bv: (layer, be[i], 0, j)),
                  pl.BlockSpec((None, None, tf, d), lambda i, j, be, bv: (layer, be[i], j, 0))],
        out_specs=pl.BlockSpec(memory_space=pl.ANY),
        scratch_shapes=[pltpu.VMEM((2, rows * SUBLANES, LANES), F32), pltpu.VMEM((rows, d), BF16),
                        pltpu.VMEM((rows, d), F32), pltpu.VMEM((rows * SUBLANES, LANES), F32),
                        pltpu.SemaphoreType.DMA((2,)), pltpu.SemaphoreType.DMA(())])
    return pl.pallas_call(
        kern,
        grid_spec=grid_spec,
        out_shape=jax.ShapeDtypeStruct(((TOP_K * n + rows) * SUBLANES, LANES), F32),
        compiler_params=_params("arbitrary", "arbitrary"),
        name="moe_ffn",
    )(block_e, block_valid, slot_tok, slot_tok, slot_dst, slot_gate, xn, wg, wu, wd)


def _combine_tiles(h_ref, y0_ref, y1_ref):
    tm, d = h_ref.shape
    pieces = d // LANES
    y0 = _from_token_tiles(y0_ref, tm, pieces)
    y1 = _from_token_tiles(y1_ref, tm, pieces)
    return jnp.concatenate([y0[s] + y1[s] for s in range(pieces)], axis=1) + h_ref[...]


def _combine_kernel(h_ref, y0_ref, y1_ref, g_ref, o_ref, *, norm):
    x = _combine_tiles(h_ref, y0_ref, y1_ref)
    o_ref[...] = _rms(x, g_ref[...]) if norm else x


def moe_combine(h, routed, final_gain=None, tm=512):
    n, d = h.shape
    tm = min(tm, n)
    nt = n // tm
    norm = final_gain is not None
    gain = final_gain if norm else jnp.ones((d,), F32)
    return pl.pallas_call(
        functools.partial(_combine_kernel, norm=norm),
        grid=(nt,),
        in_specs=[pl.BlockSpec((tm, d), lambda i: (i, 0)),
                  pl.BlockSpec((tm * SUBLANES, LANES), lambda i: (i, 0)),
                  pl.BlockSpec((tm * SUBLANES, LANES), lambda i: (nt + i, 0)),
                  pl.BlockSpec((1, d), lambda i: (0, 0))],
        out_specs=pl.BlockSpec((tm, d), lambda i: (i, 0)),
        out_shape=jax.ShapeDtypeStruct((n, d), F32),
        compiler_params=_params("parallel"),
        name="moe_combine",
    )(h, routed, routed, gain.reshape(1, d))


def _final_norm_kernel(h_ref, g_ref, o_ref):
    o_ref[...] = _rms(h_ref[...], g_ref[...])


def final_norm(h, gain, tm=512):
    n, d = h.shape
    tm = min(tm, n)
    return pl.pallas_call(
        _final_norm_kernel,
        grid=(n // tm,),
        in_specs=[pl.BlockSpec((tm, d), lambda i: (i, 0)), pl.BlockSpec((1, d), lambda i: (0, 0))],
        out_specs=pl.BlockSpec((tm, d), lambda i: (i, 0)),
        out_shape=jax.ShapeDtypeStruct((n, d), F32),
        compiler_params=_params("parallel"),
        name="final_norm",
    )(h, gain.reshape(1, d))


def kernel(x, norm_mix, norm_ffn, norm_final, ret_w_in, ret_gn, ret_w_out, s5_w_in, s5_a_re, s5_a_im, s5_log_step, s5_b_re, s5_b_im, s5_c_re, s5_c_im, s5_d, s5_w_out, moba_w_in, moba_w_out, ffn_w_gate, ffn_w_up, ffn_w_down, moe_router, moe_w_gate, moe_w_up, moe_w_down):
    batch, seq, d = x.shape
    depth = norm_mix.shape[0]
    n = batch * seq
    h = x.reshape(n, d).astype(F32)
    i_ret = i_s5 = i_moba = i_dense = i_moe = 0
    out = None
    for i in range(depth):
        mixer = i % N_MIXERS
        moe = i % 2 == 1
        route = (norm_ffn[i], moe_router, i_moe) if moe else None
        if mixer == 0:
            proj = norm_matmul(h, norm_mix[i], ret_w_in, i_ret, BF16)
            y = retention_core(proj, ret_gn[i_ret], batch, seq)
            res = out_projection(y, ret_w_out, i_ret, h, route=route)
            i_ret += 1
        elif mixer == 1:
            sub = math.gcd(seq, S5_SUBCHUNK)
            u = norm_matmul(h, norm_mix[i], s5_w_in, i_s5, F32)
            ops = _s5_operators(s5_a_re[i_s5], s5_a_im[i_s5], s5_log_step[i_s5], s5_b_re[i_s5],
                                s5_b_im[i_s5], s5_c_re[i_s5], s5_c_im[i_s5], sub)
            y = s5_core(u, s5_d[i_s5], ops, batch, seq, sub)
            res = out_projection(y, s5_w_out, i_s5, h, glu=True, route=route)
            i_s5 += 1
        else:
            qkv = norm_matmul(h, norm_mix[i], moba_w_in, i_moba, BF16)
            o = moba_core(qkv, batch, seq)
            res = out_projection(o, moba_w_out, i_moba, h, route=route)
            i_moba += 1
        last = i == depth - 1
        if not moe:
            h = dense_ffn(res, norm_ffn[i], ffn_w_gate, ffn_w_up, ffn_w_down, i_dense)
            i_dense += 1
            if last:
                out = final_norm(h, norm_final)
        else:
            h, xn, logits = res
            tables = _route_tables(logits, min(MOE_ROWS, n))
            routed = moe_ffn(xn, tables, moe_w_gate, moe_w_up, moe_w_down, i_moe)
            i_moe += 1
            h = moe_combine(h, routed, norm_final if last else None)
            if last:
                out = h
    return out.reshape(batch, seq, d).astype(x.dtype)
```

```python
import functools
import math

import numpy as np
import jax
import jax.numpy as jnp
from jax import lax
from jax.experimental import pallas as pl
from jax.experimental.pallas import tpu as pltpu

F32 = jnp.float32
BF16 = jnp.bfloat16

NORM_EPS = 1e-6
NEG_INF = -1e30
N_MIXERS = 3

RET_HEADS = 4
RET_CHUNK = 256
S5_GROUP = 16
S5_LANE_GROUPS = 8
S5_SUBCHUNK = 16
S5_ROWS = 128
S5_DT_MIN = 1e-3
MOBA_HEADS = 8
MOBA_BLOCK = 256
MOBA_TOPK = 3
MOBA_GROUP = 4
MOBA_QUERY_BLOCKS = 4
TOP_K = 2
MOE_ROWS = 1024
MOE_ISSUE_UNROLL = 8

V7X_VMEM_LIMIT_BYTES = 56 * 1024 * 1024
LANES = 128
SUBLANES = 8


def _params(*sem):
    return pltpu.CompilerParams(dimension_semantics=sem, vmem_limit_bytes=V7X_VMEM_LIMIT_BYTES)


def _rms(x, gain):
    return x * lax.rsqrt(jnp.mean(x * x, axis=-1, keepdims=True) + NORM_EPS) * gain


def _dot(a, b):
    return jnp.dot(a, b, preferred_element_type=F32)


def _dot_nt(a, b):
    return lax.dot_general(a, b, (((1,), (1,)), ((), ())), preferred_element_type=F32)


def _dot_tn(a, b):
    return lax.dot_general(a, b, (((0,), (0,)), ((), ())), preferred_element_type=F32)


def _norm_matmul_kernel(x_ref, g_ref, w_ref, o_ref, xn_ref):
    @pl.when(pl.program_id(1) == 0)
    def _():
        xn_ref[...] = _rms(x_ref[...], g_ref[...]).astype(BF16)

    o_ref[...] = _dot(xn_ref[...], w_ref[...]).astype(o_ref.dtype)


def norm_matmul(x, gain, w, layer, out_dtype, tm=1024, tn=1024):
    m, d = x.shape
    n = w.shape[2]
    tm, tn = min(tm, m), min(tn, n)
    return pl.pallas_call(
        _norm_matmul_kernel,
        grid=(m // tm, n // tn),
        in_specs=[pl.BlockSpec((tm, d), lambda i, j: (i, 0)),
                  pl.BlockSpec((1, d), lambda i, j: (0, 0)),
                  pl.BlockSpec((d, tn), lambda i, j: (0, j))],
        out_specs=pl.BlockSpec((tm, tn), lambda i, j: (i, j)),
        out_shape=jax.ShapeDtypeStruct((m, n), out_dtype),
        scratch_shapes=[pltpu.VMEM((tm, d), BF16)],
        compiler_params=_params("parallel", "arbitrary"),
        name="norm_matmul",
    )(x, gain.reshape(1, d), w[layer].astype(BF16))


def _out_proj_kernel(*refs, n_w, route):
    a_ref, w_refs, r_ref = refs[0], refs[1:1 + n_w], refs[1 + n_w]
    rest = refs[2 + n_w:]
    if route:
        g_ref, rw_ref, o_ref, xn_ref, lg_ref = rest[:5]
        wb_refs = rest[5:]
    else:
        o_ref, wb_refs = rest[0], rest[1:]

    @pl.when(pl.program_id(0) == 0)
    def _():
        for w_ref, wb_ref in zip(w_refs, wb_refs):
            wb_ref[...] = w_ref[...].astype(BF16)

    a = a_ref[...].astype(BF16)
    z = _dot(a, wb_refs[0][...])
    if n_w == 2:
        z = z * jax.nn.sigmoid(_dot(a, wb_refs[1][...]))
    h = r_ref[...] + z
    o_ref[...] = h
    if route:
        xn = _rms(h, g_ref[...])
        _to_token_tiles(xn_ref, xn)
        lg_ref[...] = jnp.concatenate(
            [jnp.sum(xn * rw_ref[e:e + 1, :], axis=1, keepdims=True) for e in range(rw_ref.shape[0])], axis=1)


def out_projection(a, w, layer, res, glu=False, route=None, tm=512):
    m, k = a.shape
    n_w = 2 if glu else 1
    n = w.shape[2] // n_w
    tm = min(tm, m)
    in_specs = [pl.BlockSpec((tm, k), lambda i: (i, 0))]
    in_specs += [pl.BlockSpec((None, k, n), lambda i, c=c: (layer, 0, c)) for c in range(n_w)]
    in_specs.append(pl.BlockSpec((tm, n), lambda i: (i, 0)))
    args = [a] + [w] * n_w + [res]
    out_specs = [pl.BlockSpec((tm, n), lambda i: (i, 0))]
    out_shape = [jax.ShapeDtypeStruct((m, n), F32)]
    if route:
        gain, w_router, r_layer = route
        e = w_router.shape[2]
        assert n == SUBLANES * LANES
        in_specs += [pl.BlockSpec((1, n), lambda i: (0, 0)), pl.BlockSpec((e, n), lambda i: (0, 0))]
        args += [gain.reshape(1, n), w_router[r_layer].astype(F32).T]
        out_specs += [pl.BlockSpec((tm * SUBLANES, LANES), lambda i: (i, 0)), pl.BlockSpec((tm, e), lambda i: (i, 0))]
        out_shape += [jax.ShapeDtypeStruct((m * SUBLANES, LANES), F32), jax.ShapeDtypeStruct((m, e), F32)]
    out = pl.pallas_call(
        functools.partial(_out_proj_kernel, n_w=n_w, route=bool(route)),
        grid=(m // tm,),
        in_specs=in_specs,
        out_specs=out_specs,
        out_shape=out_shape,
        scratch_shapes=[pltpu.VMEM((k, n), BF16)] * n_w,
        compiler_params=_params("arbitrary"),
        name="out_projection",
    )(*args)
    return out if route else out[0]


def _swiglu_step(xn, wg_ref, wu_ref, wd_ref):
    g = _dot(xn, wg_ref[...].astype(BF16))
    u = _dot(xn, wu_ref[...].astype(BF16))
    a = (g * jax.nn.sigmoid(g) * u).astype(BF16)
    return _dot(a, wd_ref[...].astype(BF16))


def _ffn_kernel(x_ref, g_ref, wg_ref, wu_ref, wd_ref, o_ref, xn_ref, acc_ref):
    j = pl.program_id(1)

    @pl.when(j == 0)
    def _():
        xn_ref[...] = _rms(x_ref[...], g_ref[...]).astype(BF16)
        acc_ref[...] = jnp.zeros_like(acc_ref)

    acc_ref[...] += _swiglu_step(xn_ref[...], wg_ref, wu_ref, wd_ref)

    @pl.when(j == pl.num_programs(1) - 1)
    def _():
        o_ref[...] = x_ref[...] + acc_ref[...]


def dense_ffn(x, gain, wg, wu, wd, layer, tm=1024, tf=512):
    m, d = x.shape
    f = wg.shape[2]
    tm, tf = min(tm, m), min(tf, f)
    return pl.pallas_call(
        _ffn_kernel,
        grid=(m // tm, f // tf),
        in_specs=[pl.BlockSpec((tm, d), lambda i, j: (i, 0)),
                  pl.BlockSpec((1, d), lambda i, j: (0, 0)),
                  pl.BlockSpec((None, d, tf), lambda i, j: (layer, 0, j)),
                  pl.BlockSpec((None, d, tf), lambda i, j: (layer, 0, j)),
                  pl.BlockSpec((None, tf, d), lambda i, j: (layer, j, 0))],
        out_specs=pl.BlockSpec((tm, d), lambda i, j: (i, 0)),
        out_shape=jax.ShapeDtypeStruct((m, d), F32),
        scratch_shapes=[pltpu.VMEM((tm, d), BF16), pltpu.VMEM((tm, d), F32)],
        compiler_params=_params("parallel", "arbitrary"),
        name="dense_ffn",
    )(x, gain.reshape(1, d), wg, wu, wd)


def _retention_kernel(q_ref, k_ref, v_ref, g_ref, dec_ref, qd_ref, kd_ref, gn_ref, o_ref, r_ref,
                      *, heads, dk, dv, chunk_decay):
    @pl.when(pl.program_id(1) == 0)
    def _():
        r_ref[...] = jnp.zeros_like(r_ref)

    for h in range(heads):
        qh = q_ref[:, h * dk:(h + 1) * dk]
        kh = k_ref[:, h * dk:(h + 1) * dk]
        vh = v_ref[:, h * dv:(h + 1) * dv]
        s = _dot_nt(qh, kh) * dec_ref[h]
        y = _dot(s.astype(BF16), vh)
        y = y + _dot(qh, r_ref[h].astype(BF16)) * qd_ref[h]
        kd = (kh.astype(F32) * kd_ref[h]).astype(BF16)
        r_ref[h] = r_ref[h] * chunk_decay[h] + _dot_tn(kd, vh)
        mu = jnp.mean(y, axis=-1, keepdims=True)
        yc = y - mu
        var = jnp.mean(yc * yc, axis=-1, keepdims=True)
        yn = yc * lax.rsqrt(var + NORM_EPS) * gn_ref[:, h * dv:(h + 1) * dv]
        gate = g_ref[:, h * dv:(h + 1) * dv].astype(F32)
        o_ref[:, h * dv:(h + 1) * dv] = (yn * (gate * jax.nn.sigmoid(gate))).astype(o_ref.dtype)


def retention_core(proj, gn_gain, batch, seq):
    d = proj.shape[1] // 6
    heads = RET_HEADS
    dk, dv = d // heads, 2 * d // heads
    c = math.gcd(seq, RET_CHUNK)
    nc = seq // c
    log_gamma = np.log1p(-np.exp2(-5.0 - np.arange(heads, dtype=np.float64)))
    idx = np.arange(c, dtype=np.float64)
    diff = idx[:, None] - idx[None, :]
    scale = dk ** -0.5
    decay = np.where(diff >= 0, np.exp(log_gamma[:, None, None] * np.maximum(diff, 0.0)), 0.0) * scale
    q_decay = np.exp(log_gamma[:, None] * (idx + 1.0))[:, :, None]
    k_decay = np.exp(log_gamma[:, None] * (c - 1.0 - idx))[:, :, None] * scale
    chunk_decay = tuple(float(v) for v in np.exp(log_gamma * c))
    kern = functools.partial(_retention_kernel, heads=heads, dk=dk, dv=dv, chunk_decay=chunk_decay)
    row = lambda b, n: b * nc + n
    return pl.pallas_call(
        kern,
        grid=(batch, nc),
        in_specs=[pl.BlockSpec((c, d), lambda b, n: (row(b, n), 0)),
                  pl.BlockSpec((c, d), lambda b, n: (row(b, n), 1)),
                  pl.BlockSpec((c, 2 * d), lambda b, n: (row(b, n), 1)),
                  pl.BlockSpec((c, 2 * d), lambda b, n: (row(b, n), 2)),
                  pl.BlockSpec((heads, c, c), lambda b, n: (0, 0, 0)),
                  pl.BlockSpec((heads, c, 1), lambda b, n: (0, 0, 0)),
                  pl.BlockSpec((heads, c, 1), lambda b, n: (0, 0, 0)),
                  pl.BlockSpec((1, 2 * d), lambda b, n: (0, 0))],
        out_specs=pl.BlockSpec((c, 2 * d), lambda b, n: (row(b, n), 0)),
        out_shape=jax.ShapeDtypeStruct((batch * seq, 2 * d), BF16),
        scratch_shapes=[pltpu.VMEM((heads, dk, dv), F32)],
        compiler_params=_params("parallel", "arbitrary"),
        name="retention_core",
    )(proj, proj, proj, proj, jnp.asarray(decay, F32), jnp.asarray(q_decay, F32),
      jnp.asarray(k_decay, F32), gn_gain.reshape(1, 2 * d).astype(F32))


def _gelu_tanh(x):
    return 0.5 * x * (1.0 + jnp.tanh(math.sqrt(2.0 / math.pi) * (x + 0.044715 * (x * x * x))))


def _s5_expand(tc_ref, sc_ref, qc_ref, es_ref, eq_ref, m1_ref, m2_ref, *, sub, ga, half):
    width = sub * LANES
    n_ch = LANES // ga
    p = half // ga

    def keep_own_group(x, row_group, col_group):
        return jnp.where(row_group == col_group, x, 0.0).astype(BF16)

    lag = _dot(tc_ref[0], eq_ref[...])
    ra = lax.broadcasted_iota(jnp.int32, lag.shape, 0) // n_ch
    cb = (lax.broadcasted_iota(jnp.int32, lag.shape, 1) % LANES) // n_ch
    lag = keep_own_group(lag, ra, cb)
    for s in range(sub):
        rs = slice(s * LANES, (s + 1) * LANES)
        if s:
            m1_ref[rs, 0:s * LANES] = jnp.zeros((LANES, s * LANES), BF16)
        m1_ref[rs, s * LANES:width] = lag[:, 0:(sub - s) * LANES]
        sx = _dot(sc_ref[0, rs, :], es_ref[...])
        ra = lax.broadcasted_iota(jnp.int32, sx.shape, 0) // n_ch
        cb = (lax.broadcasted_iota(jnp.int32, sx.shape, 1) % half) // p
        m1_ref[rs, width:width + 2 * half] = keep_own_group(sx, ra, cb)
    for r in range(2 * ga):
        qx = _dot(qc_ref[0, r * p:(r + 1) * p, :], eq_ref[...])
        cb = (lax.broadcasted_iota(jnp.int32, qx.shape, 1) % LANES) // n_ch
        m2_ref[r * p:(r + 1) * p, :] = keep_own_group(qx, r % ga, cb)


def _s5_kernel(u_ref, d_ref, tc_ref, sc_ref, qc_ref, es_ref, eq_ref, al_ref, o_ref,
               m1_ref, m2_ref, xl_ref, xp_ref, st_ref, *, sub, rows, steps_per_seq, ga):
    half = st_ref.shape[1] // 2
    width = sub * LANES

    @pl.when(pl.program_id(1) == 0)
    def _():
        _s5_expand(tc_ref, sc_ref, qc_ref, es_ref, eq_ref, m1_ref, m2_ref, sub=sub, ga=ga, half=half)

    @pl.when(pl.program_id(1) % steps_per_seq == 0)
    def _():
        st_ref[...] = jnp.zeros_like(st_ref)

    us = [u_ref[pl.ds(t, rows, stride=sub), :] for t in range(sub)]
    ucat = jnp.concatenate([u.astype(BF16) for u in us], axis=1)
    y_all = _dot(ucat, m1_ref[...])
    xl_ref[...] = y_all[:, width:]

    a_re = al_ref[0, 0:1, :]
    a_im = al_ref[0, 1:2, :]

    def step(c, carry):
        xr, xi = carry
        xp_ref[pl.ds(c, 1), 0:half] = xr
        xp_ref[pl.ds(c, 1), half:2 * half] = xi
        lr = xl_ref[pl.ds(c, 1), 0:half]
        li = xl_ref[pl.ds(c, 1), half:2 * half]
        return a_re * xr - a_im * xi + lr, a_re * xi + a_im * xr + li

    xr, xi = lax.fori_loop(0, rows, step, (st_ref[0:1, 0:half], st_ref[0:1, half:2 * half]))
    st_ref[0:1, 0:half] = xr
    st_ref[0:1, half:2 * half] = xi

    y_cross = _dot(xp_ref[...].astype(BF16), m2_ref[...])
    skip = d_ref[...]
    for t in range(sub):
        y = y_all[:, t * LANES:(t + 1) * LANES] + y_cross[:, t * LANES:(t + 1) * LANES] + skip * us[t]
        o_ref[pl.ds(t, rows, stride=sub), :] = _gelu_tanh(y)


def _s5_operators(a_re, a_im, log_step, b_re, b_im, c_re, c_im, sub):
    g, p = a_re.shape
    n = b_re.shape[2]
    ga = S5_LANE_GROUPS
    j = g // ga
    dt = jnp.exp(log_step.astype(F32))[:, None]
    ar, ai = a_re.astype(F32), a_im.astype(F32)
    mag = jnp.exp(ar * dt)
    abar_re = mag * jnp.cos(ai * dt)
    abar_im = mag * jnp.sin(ai * dt)
    den = ar * ar + ai * ai
    nr, ni = abar_re - 1.0, abar_im
    f_re = (nr * ar + ni * ai) / den
    f_im = (ni * ar - nr * ai) / den
    br, bi = b_re.astype(F32), b_im.astype(F32)
    bb_re = f_re[..., None] * br - f_im[..., None] * bi
    bb_im = f_re[..., None] * bi + f_im[..., None] * br
    cr, ci = c_re.astype(F32), c_im.astype(F32)
    tau = jnp.arange(sub + 1, dtype=F32)[:, None, None]
    pw_mag = jnp.exp(tau * (ar * dt)[None])
    pw_re = pw_mag * jnp.cos(tau * (ai * dt)[None])
    pw_im = pw_mag * jnp.sin(tau * (ai * dt)[None])
    hp = lax.Precision.HIGHEST
    ab_re = pw_re[..., None] * bb_re[None] - pw_im[..., None] * bb_im[None]
    ab_im = pw_re[..., None] * bb_im[None] + pw_im[..., None] * bb_re[None]
    lag = (jnp.einsum('tgpm,gnp->tgmn', ab_re[:sub], cr, precision=hp)
           - jnp.einsum('tgpm,gnp->tgmn', ab_im[:sub], ci, precision=hp))
    tc = lag.reshape(sub, j, ga, n, n).transpose(1, 2, 3, 0, 4).reshape(j, ga * n, sub * n)
    rev = sub - 1 - jnp.arange(sub)
    sc = jnp.stack([ab_re[rev], ab_im[rev]], axis=0).reshape(2, sub, j, ga, p, n)
    sc = sc.transpose(2, 1, 3, 5, 0, 4).reshape(j, sub * ga * n, 2 * p)
    q_re = cr[None] * pw_re[1:][:, :, None, :] - ci[None] * pw_im[1:][:, :, None, :]
    q_im = -(cr[None] * pw_im[1:][:, :, None, :] + ci[None] * pw_re[1:][:, :, None, :])
    qc = jnp.stack([q_re, q_im], axis=0).reshape(2, sub, j, ga, n, p)
    qc = qc.transpose(2, 0, 3, 5, 1, 4).reshape(j, 2 * ga * p, sub * n)
    al = jnp.stack([pw_re[sub].reshape(j, ga * p), pw_im[sub].reshape(j, ga * p)], axis=1)
    return tc.astype(BF16), sc.astype(BF16), qc.astype(BF16), al


def _s5_spread_matrices(sub, ga, n, p):
    half = ga * p
    r = np.arange(2 * p)
    c = np.arange(2 * half)
    es = (r[:, None] // p == c[None, :] // half) & (r[:, None] % p == c[None, :] % p)
    r = np.arange(sub * n)
    c = np.arange(sub * ga * n)
    eq = (r[:, None] // n == c[None, :] // (ga * n)) & (r[:, None] % n == c[None, :] % n)
    return jnp.asarray(es, BF16), jnp.asarray(eq, BF16)


def s5_core(u, d_skip, ops, batch, seq, sub):
    tc, sc, qc, al = ops
    n_tok, w = u.shape
    j = tc.shape[0]
    half = al.shape[2]
    ga = S5_LANE_GROUPS
    es, eq = _s5_spread_matrices(sub, ga, LANES // ga, half // ga)
    rows = min(S5_ROWS, seq // sub)
    step_tok = rows * sub
    steps_per_seq = seq // step_tok
    width = sub * LANES
    kern = functools.partial(_s5_kernel, sub=sub, rows=rows, steps_per_seq=steps_per_seq, ga=ga)
    tile = lambda a: pl.BlockSpec((1,) + a.shape[1:], lambda jj, i: (jj, 0, 0))
    whole = lambda a: pl.BlockSpec(a.shape, lambda jj, i: (0, 0))
    return pl.pallas_call(
        kern,
        grid=(j, n_tok // step_tok),
        in_specs=[pl.BlockSpec((step_tok, LANES), lambda jj, i: (i, jj)),
                  pl.BlockSpec((1, LANES), lambda jj, i: (0, jj)),
                  tile(tc), tile(sc), tile(qc), whole(es), whole(eq), tile(al)],
        out_specs=pl.BlockSpec((step_tok, LANES), lambda jj, i: (i, jj)),
        out_shape=jax.ShapeDtypeStruct((n_tok, w), F32),
        scratch_shapes=[pltpu.VMEM((width, width + 2 * half), BF16), pltpu.VMEM((2 * half, width), BF16),
                        pltpu.VMEM((rows, 2 * half), F32), pltpu.VMEM((rows, 2 * half), F32),
                        pltpu.VMEM((8, 2 * half), F32)],
        compiler_params=_params("parallel", "arbitrary"),
        name="s5_core",
    )(u, d_skip.reshape(1, w).astype(F32), tc, sc, qc, es, eq, al)


MOBA_AUX_CONSTS = 6


def _moba_kernel(cst_ref, q_ref, k_ref, v_ref, o_ref, km_ref, ka_ref, vt_ref, sa_ref, sb_ref, acc_ref,
                 *, blk, topk, scale, group, qblocks):
    h = pl.program_id(1)
    i = pl.program_id(2)
    first = i * qblocks
    nb = k_ref.shape[0] // blk
    hd = q_ref.shape[1]
    aux = ka_ref.shape[2] - hd
    nc = MOBA_AUX_CONSTS

    @pl.when(i == 0)
    def _():
        km_ref[...] = jnp.zeros_like(km_ref)
        jl = lax.broadcasted_iota(jnp.int32, (blk, aux), 0)
        lane = lax.broadcasted_iota(jnp.int32, (blk, aux), 1)

        def fill(n, c):
            st = pl.multiple_of(n * blk, blk)
            kb = k_ref[pl.ds(st, blk), :]
            km_ref[pl.ds(n + nc, 1), :] = jnp.mean(kb.astype(F32), axis=0, keepdims=True)
            pat = jnp.where(lane < nc // 2, n, jnp.where(lane < nc, jl, (lane - nc == n).astype(jnp.int32)))
            ka_ref[n, :, 0:hd] = kb
            ka_ref[n, :, hd:hd + aux] = pat.astype(F32).astype(BF16)
            vt_ref[n] = v_ref[pl.ds(st, blk), :].astype(F32).T.astype(BF16)
            return c

        lax.fori_loop(0, nb, fill, 0)

    q = q_ref[...]
    rest = km_ref[...]
    gate = None
    for _ in range(3):
        piece = rest.astype(BF16)
        rest = rest - piece.astype(F32)
        part = _dot_nt(q, piece)
        gate = part if gate is None else gate + part
    lane = lax.broadcasted_iota(jnp.int32, gate.shape, 1)
    own = first + lax.broadcasted_iota(jnp.int32, gate.shape, 0) // blk
    gate = jnp.where(jnp.logical_and(lane >= nc, lane - nc < own), gate, NEG_INF)
    chosen = jnp.zeros(gate.shape, jnp.bool_)
    lane_f = lane.astype(F32)
    for _ in range(topk):
        mx = jnp.max(gate, axis=1, keepdims=True)
        idx = jnp.min(jnp.where(gate == mx, lane_f, float(2 ** 20)), axis=1, keepdims=True)
        hit = lane_f == idx
        chosen = jnp.logical_or(chosen, jnp.logical_and(hit, mx > 0.5 * NEG_INF))
        gate = jnp.where(hit, NEG_INF, gate)

    qx = jnp.where(chosen, 0.0, NEG_INF)
    for c in range(nc):
        qx = jnp.where(lane == c, cst_ref[h, c], qx)
    qa = jnp.concatenate([q, qx.astype(BF16)], axis=1)
    a_full = cst_ref[h, 0] + cst_ref[h, 1] + cst_ref[h, 2]
    c_full = cst_ref[h, 3] + cst_ref[h, 4] + cst_ref[h, 5]

    jk = lax.broadcasted_iota(jnp.int32, (blk, blk), 0)
    jq = lax.broadcasted_iota(jnp.int32, (blk, blk), 1)
    c2 = scale * math.log2(math.e)
    m0, l0, acc0 = [], [], []
    for w in range(qblocks):
        bw = first + w
        st = _dot_nt(ka_ref[bw, :, 0:hd], q[w * blk:(w + 1) * blk])
        st = st + (c_full * jk.astype(F32) + a_full * bw.astype(F32))
        st = jnp.where(jk <= jq, st, NEG_INF)
        mw = jnp.max(st, axis=0, keepdims=True)
        p = jnp.exp2((st - mw) * c2)
        m0.append(mw)
        l0.append(jnp.sum(p, axis=0, keepdims=True))
        acc0.append(_dot(vt_ref[bw], p.astype(BF16)))
    m0, l0, acc0 = (jnp.concatenate(x, axis=1) for x in (m0, l0, acc0))

    n_groups = nb // group

    half = group // 2

    def scores(g, s_ref):
        tops = []
        for u in range(2):
            ka = ka_ref[pl.ds(g * group + u * half, half)].reshape(half * blk, hd + aux)
            st = _dot_nt(ka, qa)
            s_ref[u * half * blk:(u + 1) * half * blk, :] = st
            tops.append(jnp.max(st, axis=0, keepdims=True))
        return jnp.maximum(tops[0], tops[1])

    def softmax(st, top, m, l):
        m_new = jnp.maximum(m, top)
        alpha = jnp.exp2((m - m_new) * c2)
        p = jnp.exp2((st - m_new) * c2)
        return m_new, alpha * l + jnp.sum(p, axis=0, keepdims=True), alpha, p.astype(BF16)

    def weighted_values(g, pb):
        pv = _dot(vt_ref[g * group], pb[0:blk])
        for u in range(1, group):
            pv = pv + _dot(vt_ref[g * group + u], pb[u * blk:(u + 1) * blk])
        return pv

    top0 = scores(0, sa_ref)
    acc_ref[...] = acc0
    past = first + qblocks - 1

    def body(t, carry):
        m, l, top_a = carry
        g0 = 2 * t
        g1 = g0 + 1
        g2 = jnp.minimum(g0 + 2, n_groups - 1)
        top_b = scores(g1, sb_ref)
        m, l, alpha, pa = softmax(sa_ref[...], top_a, m, l)
        acc_ref[...] = alpha * acc_ref[...] + weighted_values(g0, pa)
        top_a = scores(g2, sa_ref)
        m, l, alpha, pb = softmax(sb_ref[...], top_b, m, l)
        acc_ref[...] = alpha * acc_ref[...] + weighted_values(g1, pb)
        return m, l, top_a

    m, l, _ = lax.fori_loop(0, (past + 2 * group - 1) // (2 * group), body, (m0, l0, top0))
    o_ref[...] = (acc_ref[...] / l).T.astype(o_ref.dtype)


def moba_core(qkv, batch, seq):
    d = qkv.shape[1] // 3
    heads = MOBA_HEADS
    hd = d // heads
    blk = MOBA_BLOCK
    assert seq % blk == 0
    nb = seq // blk
    topk = max(1, min(MOBA_TOPK, nb - 1))
    group = math.gcd(nb, MOBA_GROUP)
    aux = LANES
    assert MOBA_AUX_CONSTS + nb <= aux and (nb // group) % 2 == 0
    scale = hd ** -0.5
    slopes = np.exp2(-8.0 * (np.arange(heads, dtype=np.float64) + 1.0) / heads)
    consts = []
    for val in (slopes * blk / scale, slopes / scale):
        rest = jnp.asarray(val, F32)
        for _ in range(MOBA_AUX_CONSTS // 2):
            piece = rest.astype(BF16).astype(F32)
            consts.append(piece)
            rest = rest - piece
    cst = jnp.stack(consts, axis=1)
    qblocks = math.gcd(nb, MOBA_QUERY_BLOCKS)
    nq = nb // qblocks
    kern = functools.partial(_moba_kernel, blk=blk, topk=topk, scale=scale, group=group, qblocks=qblocks)
    return pl.pallas_call(
        kern,
        grid=(batch, heads, nq),
        in_specs=[pl.BlockSpec(memory_space=pltpu.SMEM),
                  pl.BlockSpec((qblocks * blk, hd), lambda b, h, i: (b * nq + i, h)),
                  pl.BlockSpec((seq, hd), lambda b, h, i: (b, heads + h)),
                  pl.BlockSpec((seq, hd), lambda b, h, i: (b, 2 * heads + h))],
        out_specs=pl.BlockSpec((qblocks * blk, hd), lambda b, h, i: (b * nq + i, h)),
        out_shape=jax.ShapeDtypeStruct((batch * seq, d), BF16),
        scratch_shapes=[pltpu.VMEM((aux, hd), F32), pltpu.VMEM((nb, blk, hd + aux), BF16),
                        pltpu.VMEM((nb, hd, blk), BF16),
                        pltpu.VMEM((group * blk, qblocks * blk), F32),
                        pltpu.VMEM((group * blk, qblocks * blk), F32),
                        pltpu.VMEM((hd, qblocks * blk), F32)],
        compiler_params=_params("parallel", "parallel", "arbitrary"),
        name="moba_core",
    )(cst, qkv, qkv, qkv)


def _to_token_tiles(ref, x):
    pieces = x.shape[1] // LANES
    for s in range(pieces):
        ref[pl.ds(s, x.shape[0], stride=pieces), :] = x[:, s * LANES:(s + 1) * LANES]


def _from_token_tiles(ref, tokens, pieces):
    return [ref[pl.ds(s, tokens, stride=pieces), :] for s in range(pieces)]


def _route_tables(logits, rows):
    n, e = logits.shape
    n_pairs = n * TOP_K
    i32 = jnp.int32
    top_logits, top_idx = lax.top_k(logits, TOP_K)
    gates = jax.nn.softmax(top_logits, axis=-1).reshape(-1)
    e_flat = top_idx.reshape(-1).astype(i32)
    e_sorted, pair_sorted, gate_sorted = lax.sort((e_flat, jnp.arange(n_pairs, dtype=i32), gates),
                                                  num_keys=1, is_stable=True)
    experts = jnp.arange(e, dtype=i32)
    start = jnp.searchsorted(e_sorted, experts, side='left', method='compare_all').astype(i32)
    counts = jnp.searchsorted(e_sorted, experts, side='right', method='compare_all').astype(i32) - start
    padded = ((counts + rows - 1) // rows) * rows
    pend = jnp.cumsum(padded)
    pstart = pend - padded
    n_blocks = -(-n_pairs // rows) + e
    blk0 = jnp.arange(n_blocks, dtype=i32) * rows
    block_e = jnp.minimum(jnp.searchsorted(pend, blk0, side='right', method='compare_all'), e - 1).astype(i32)
    first = blk0 - pstart[block_e]
    live = jnp.clip(counts[block_e] - first, 0, rows)
    off = jnp.clip(start[block_e] + first, 0, n_pairs)
    r = jnp.arange(rows, dtype=i32)[None, :]
    idx = jnp.minimum(off[:, None] + r, n_pairs - 1)
    pair_w, gate_w = pair_sorted[idx], gate_sorted[idx]
    valid = r < live[:, None]
    tok = pair_w // TOP_K
    slot_tok = jnp.where(valid, tok, 0).astype(i32)
    slot_dst = jnp.where(valid, (pair_w % TOP_K) * n + tok, TOP_K * n + r).astype(i32)
    slot_gate = jnp.where(valid, gate_w, 0.0).astype(F32)
    block_valid = (blk0 < pend[-1]).astype(i32)
    return (block_e, block_valid, slot_tok.reshape(n_blocks, 1, rows),
            slot_dst.reshape(n_blocks, 1, rows), slot_gate.reshape(n_blocks * rows, 1))


def _moe_kernel(be_ref, bv_ref, tok_ref, tok_next_ref, dst_ref, gate_ref, xn_hbm, wg_ref, wu_ref, wd_ref, y_hbm,
                xg_ref, xb_ref, acc_ref, yb_ref, sem_in, sem_out, *, rows):
    i = pl.program_id(0)
    j = pl.program_id(1)
    last_j = pl.num_programs(1) - 1
    valid = bv_ref[i] != 0
    nxt = jnp.minimum(i + 1, pl.num_programs(0) - 1)
    next_valid = jnp.logical_and(i + 1 < pl.num_programs(0), bv_ref[nxt] != 0)
    pieces = SUBLANES

    def tile(ref, t):
        return ref.at[pl.ds(pl.multiple_of(t * pieces, pieces), pieces)]

    def gather(table_ref, slot):
        def issue(r, c):
            pltpu.make_async_copy(tile(xn_hbm, table_ref[0, 0, r]), tile(xg_ref.at[slot], r), sem_in.at[slot]).start()
            return c

        lax.fori_loop(0, rows, issue, 0, unroll=MOE_ISSUE_UNROLL)

    def gather_wait(slot):
        pltpu.make_async_copy(xg_ref.at[slot], xg_ref.at[slot], sem_in.at[slot]).wait()

    def scatter():
        def issue(r, c):
            pltpu.make_async_copy(tile(yb_ref, r), tile(y_hbm, dst_ref[0, 0, r]), sem_out).start()
            return c

        lax.fori_loop(0, rows, issue, 0, unroll=MOE_ISSUE_UNROLL)

    def scatter_wait():
        pltpu.make_async_copy(yb_ref, yb_ref, sem_out).wait()

    @pl.when(jnp.logical_and(i == 0, j == 0))
    def _():
        yb_ref[...] = jnp.zeros_like(yb_ref)
        n_spare = rows * pieces
        spare = pltpu.make_async_copy(yb_ref, y_hbm.at[pl.ds(y_hbm.shape[0] - n_spare, n_spare)], sem_out)
        spare.start()
        spare.wait()

        @pl.when(valid)
        def _():
            gather(tok_ref, 0)

    @pl.when(jnp.logical_and(valid, j == 0))
    def _():
        for slot in range(2):
            @pl.when(i % 2 == slot)
            def _():
                gather_wait(slot)
                for s, piece in enumerate(_from_token_tiles(xg_ref.at[slot], rows, pieces)):
                    xb_ref[:, s * LANES:(s + 1) * LANES] = piece.astype(BF16)

                @pl.when(next_valid)
                def _():
                    gather(tok_next_ref, 1 - slot)

        acc_ref[...] = jnp.zeros_like(acc_ref)

    @pl.when(valid)
    def _():
        acc_ref[...] += _swiglu_step(xb_ref[...], wg_ref, wu_ref, wd_ref)

    @pl.when(jnp.logical_and(valid, j == last_j))
    def _():
        @pl.when(i > 0)
        def _():
            scatter_wait()

        _to_token_tiles(yb_ref, acc_ref[...] * gate_ref[...])
        scatter()

        @pl.when(jnp.logical_not(next_valid))
        def _():
            scatter_wait()


def moe_ffn(xn, tables, wg, wu, wd, layer, tf=512):
    d, f = wg.shape[2], wg.shape[3]
    n = xn.shape[0] // SUBLANES
    block_e, block_valid, slot_tok, slot_dst, slot_gate = tables
    n_blocks, _, rows = slot_tok.shape
    tf = min(tf, f)
    kern = functools.partial(_moe_kernel, rows=rows)
    grid_spec = pltpu.PrefetchScalarGridSpec(
        num_scalar_prefetch=2,
        grid=(n_blocks, f // tf),
        in_specs=[pl.BlockSpec((1, 1, rows), lambda i, j, be, bv: (i, 0, 0), memory_space=pltpu.SMEM),
                  pl.BlockSpec((1, 1, rows), lambda i, j, be, bv: (jnp.minimum(i + 1, n_blocks - 1), 0, 0),
                               memory_space=pltpu.SMEM),
                  pl.BlockSpec((1, 1, rows), lambda i, j, be, bv: (i, 0, 0), memory_space=pltpu.SMEM),
                  pl.BlockSpec((rows, 1), lambda i, j, be, bv: (i, 0)),
                  pl.BlockSpec(memory_space=pl.ANY),
                  pl.BlockSpec((None, None, d, tf), lambda i, j, be, bv: (layer, be[i], 0, j)),
                  pl.BlockSpec((None, None, d, tf), lambda i, j, be, bv: (layer, be[i], 0, j)),
                  pl.BlockSpec((None, None, tf, d), lambda i, j, be, bv: (layer, be[i], j, 0))],
        out_specs=pl.BlockSpec(memory_space=pl.ANY),
        scratch_shapes=[pltpu.VMEM((2, rows * SUBLANES, LANES), F32), pltpu.VMEM((rows, d), BF16),
                        pltpu.VMEM((rows, d), F32), pltpu.VMEM((rows * SUBLANES, LANES), F32),
                        pltpu.SemaphoreType.DMA((2,)), pltpu.SemaphoreType.DMA(())])
    return pl.pallas_call(
        kern,
        grid_spec=grid_spec,
        out_shape=jax.ShapeDtypeStruct(((TOP_K * n + rows) * SUBLANES, LANES), F32),
        compiler_params=_params("arbitrary", "arbitrary"),
        name="moe_ffn",
    )(block_e, block_valid, slot_tok, slot_tok, slot_dst, slot_gate, xn, wg, wu, wd)


def _combine_tiles(h_ref, y0_ref, y1_ref):
    tm, d = h_ref.shape
    pieces = d // LANES
    y0 = _from_token_tiles(y0_ref, tm, pieces)
    y1 = _from_token_tiles(y1_ref, tm, pieces)
    return jnp.concatenate([y0[s] + y1[s] for s in range(pieces)], axis=1) + h_ref[...]


def _combine_kernel(h_ref, y0_ref, y1_ref, g_ref, o_ref, *, norm):
    x = _combine_tiles(h_ref, y0_ref, y1_ref)
    o_ref[...] = _rms(x, g_ref[...]) if norm else x


def moe_combine(h, routed, final_gain=None, tm=512):
    n, d = h.shape
    tm = min(tm, n)
    nt = n // tm
    norm = final_gain is not None
    gain = final_gain if norm else jnp.ones((d,), F32)
    return pl.pallas_call(
        functools.partial(_combine_kernel, norm=norm),
        grid=(nt,),
        in_specs=[pl.BlockSpec((tm, d), lambda i: (i, 0)),
                  pl.BlockSpec((tm * SUBLANES, LANES), lambda i: (i, 0)),
                  pl.BlockSpec((tm * SUBLANES, LANES), lambda i: (nt + i, 0)),
                  pl.BlockSpec((1, d), lambda i: (0, 0))],
        out_specs=pl.BlockSpec((tm, d), lambda i: (i, 0)),
        out_shape=jax.ShapeDtypeStruct((n, d), F32),
        compiler_params=_params("parallel"),
        name="moe_combine",
    )(h, routed, routed, gain.reshape(1, d))


def _final_norm_kernel(h_ref, g_ref, o_ref):
    o_ref[...] = _rms(h_ref[...], g_ref[...])


def final_norm(h, gain, tm=512):
    n, d = h.shape
    tm = min(tm, n)
    return pl.pallas_call(
        _final_norm_kernel,
        grid=(n // tm,),
        in_specs=[pl.BlockSpec((tm, d), lambda i: (i, 0)), pl.BlockSpec((1, d), lambda i: (0, 0))],
        out_specs=pl.BlockSpec((tm, d), lambda i: (i, 0)),
        out_shape=jax.ShapeDtypeStruct((n, d), F32),
        compiler_params=_params("parallel"),
        name="final_norm",
    )(h, gain.reshape(1, d))


def kernel(x, norm_mix, norm_ffn, norm_final, ret_w_in, ret_gn, ret_w_out, s5_w_in, s5_a_re, s5_a_im, s5_log_step, s5_b_re, s5_b_im, s5_c_re, s5_c_im, s5_d, s5_w_out, moba_w_in, moba_w_out, ffn_w_gate, ffn_w_up, ffn_w_down, moe_router, moe_w_gate, moe_w_up, moe_w_down):
    batch, seq, d = x.shape
    depth = norm_mix.shape[0]
    n = batch * seq
    h = x.reshape(n, d).astype(F32)
    i_ret = i_s5 = i_moba = i_dense = i_moe = 0
    out = None
    for i in range(depth):
        mixer = i % N_MIXERS
        moe = i % 2 == 1
        route = (norm_ffn[i], moe_router, i_moe) if moe else None
        if mixer == 0:
            proj = norm_matmul(h, norm_mix[i], ret_w_in, i_ret, BF16)
            y = retention_core(proj, ret_gn[i_ret], batch, seq)
            res = out_projection(y, ret_w_out, i_ret, h, route=route)
            i_ret += 1
        elif mixer == 1:
            sub = math.gcd(seq, S5_SUBCHUNK)
            u = norm_matmul(h, norm_mix[i], s5_w_in, i_s5, F32)
            ops = _s5_operators(s5_a_re[i_s5], s5_a_im[i_s5], s5_log_step[i_s5], s5_b_re[i_s5],
                                s5_b_im[i_s5], s5_c_re[i_s5], s5_c_im[i_s5], sub)
            y = s5_core(u, s5_d[i_s5], ops, batch, seq, sub)
            res = out_projection(y, s5_w_out, i_s5, h, glu=True, route=route)
            i_s5 += 1
        else:
            qkv = norm_matmul(h, norm_mix[i], moba_w_in, i_moba, BF16)
            o = moba_core(qkv, batch, seq)
            res = out_projection(o, moba_w_out, i_moba, h, route=route)
            i_moba += 1
        last = i == depth - 1
        if not moe:
            h = dense_ffn(res, norm_ffn[i], ffn_w_gate, ffn_w_up, ffn_w_down, i_dense)
            i_dense += 1
            if last:
                out = final_norm(h, norm_final)
        else:
            h, xn, logits = res
            tables = _route_tables(logits, min(MOE_ROWS, n))
            routed = moe_ffn(xn, tables, moe_w_gate, moe_w_up, moe_w_down, i_moe)
            i_moe += 1
            h = moe_combine(h, routed, norm_final if last else None)
            if last:
                out = h
    return out.reshape(batch, seq, d).astype(x.dtype)
```

```python
import functools
import math

import numpy as np
import jax
import jax.numpy as jnp
from jax import lax
from jax.experimental import pallas as pl
from jax.experimental.pallas import tpu as pltpu

F32 = jnp.float32
BF16 = jnp.bfloat16

NORM_EPS = 1e-6
NEG_INF = -1e30
N_MIXERS = 3

RET_HEADS = 4
RET_CHUNK = 256
S5_GROUP = 16
S5_LANE_GROUPS = 8
S5_SUBCHUNK = 16
S5_ROWS = 128
S5_DT_MIN = 1e-3
MOBA_HEADS = 8
MOBA_BLOCK = 256
MOBA_TOPK = 3
MOBA_GROUP = 4
MOBA_QUERY_BLOCKS = 4
TOP_K = 2
MOE_ROWS = 1024
MOE_ISSUE_UNROLL = 8

V7X_VMEM_LIMIT_BYTES = 56 * 1024 * 1024
LANES = 128
SUBLANES = 8


def _params(*sem):
    return pltpu.CompilerParams(dimension_semantics=sem, vmem_limit_bytes=V7X_VMEM_LIMIT_BYTES)


def _rms(x, gain):
    return x * lax.rsqrt(jnp.mean(x * x, axis=-1, keepdims=True) + NORM_EPS) * gain


def _dot(a, b):
    return jnp.dot(a, b, preferred_element_type=F32)


def _dot_nt(a, b):
    return lax.dot_general(a, b, (((1,), (1,)), ((), ())), preferred_element_type=F32)


def _dot_tn(a, b):
    return lax.dot_general(a, b, (((0,), (0,)), ((), ())), preferred_element_type=F32)


def _norm_matmul_kernel(x_ref, g_ref, w_ref, o_ref, xn_ref):
    @pl.when(pl.program_id(1) == 0)
    def _():
        xn_ref[...] = _rms(x_ref[...], g_ref[...]).astype(BF16)

    o_ref[...] = _dot(xn_ref[...], w_ref[...]).astype(o_ref.dtype)


def norm_matmul(x, gain, w, layer, out_dtype, tm=1024, tn=1024):
    m, d = x.shape
    n = w.shape[2]
    tm, tn = min(tm, m), min(tn, n)
    return pl.pallas_call(
        _norm_matmul_kernel,
        grid=(m // tm, n // tn),
        in_specs=[pl.BlockSpec((tm, d), lambda i, j: (i, 0)),
                  pl.BlockSpec((1, d), lambda i, j: (0, 0)),
                  pl.BlockSpec((d, tn), lambda i, j: (0, j))],
        out_specs=pl.BlockSpec((tm, tn), lambda i, j: (i, j)),
        out_shape=jax.ShapeDtypeStruct((m, n), out_dtype),
        scratch_shapes=[pltpu.VMEM((tm, d), BF16)],
        compiler_params=_params("parallel", "arbitrary"),
        name="norm_matmul",
    )(x, gain.reshape(1, d), w[layer].astype(BF16))


def _out_proj_kernel(*refs, n_w, route):
    a_ref, w_refs, r_ref = refs[0], refs[1:1 + n_w], refs[1 + n_w]
    rest = refs[2 + n_w:]
    if route:
        g_ref, rw_ref, o_ref, xn_ref, lg_ref = rest[:5]
        wb_refs = rest[5:]
    else:
        o_ref, wb_refs = rest[0], rest[1:]

    @pl.when(pl.program_id(0) == 0)
    def _():
        for w_ref, wb_ref in zip(w_refs, wb_refs):
            wb_ref[...] = w_ref[...].astype(BF16)

    a = a_ref[...].astype(BF16)
    z = _dot(a, wb_refs[0][...])
    if n_w == 2:
        z = z * jax.nn.sigmoid(_dot(a, wb_refs[1][...]))
    h = r_ref[...] + z
    o_ref[...] = h
    if route:
        xn = _rms(h, g_ref[...])
        _to_token_tiles(xn_ref, xn)
        lg_ref[...] = jnp.concatenate(
            [jnp.sum(xn * rw_ref[e:e + 1, :], axis=1, keepdims=True) for e in range(rw_ref.shape[0])], axis=1)


def out_projection(a, w, layer, res, glu=False, route=None, tm=512):
    m, k = a.shape
    n_w = 2 if glu else 1
    n = w.shape[2] // n_w
    tm = min(tm, m)
    in_specs = [pl.BlockSpec((tm, k), lambda i: (i, 0))]
    in_specs += [pl.BlockSpec((None, k, n), lambda i, c=c: (layer, 0, c)) for c in range(n_w)]
    in_specs.append(pl.BlockSpec((tm, n), lambda i: (i, 0)))
    args = [a] + [w] * n_w + [res]
    out_specs = [pl.BlockSpec((tm, n), lambda i: (i, 0))]
    out_shape = [jax.ShapeDtypeStruct((m, n), F32)]
    if route:
        gain, w_router, r_layer = route
        e = w_router.shape[2]
        assert n == SUBLANES * LANES
        in_specs += [pl.BlockSpec((1, n), lambda i: (0, 0)), pl.BlockSpec((e, n), lambda i: (0, 0))]
        args += [gain.reshape(1, n), w_router[r_layer].astype(F32).T]
        out_specs += [pl.BlockSpec((tm * SUBLANES, LANES), lambda i: (i, 0)), pl.BlockSpec((tm, e), lambda i: (i, 0))]
        out_shape += [jax.ShapeDtypeStruct((m * SUBLANES, LANES), F32), jax.ShapeDtypeStruct((m, e), F32)]
    out = pl.pallas_call(
        functools.partial(_out_proj_kernel, n_w=n_w, route=bool(route)),
        grid=(m // tm,),
        in_specs=in_specs,
        out_specs=out_specs,
        out_shape=out_shape,
        scratch_shapes=[pltpu.VMEM((k, n), BF16)] * n_w,
        compiler_params=_params("arbitrary"),
        name="out_projection",
    )(*args)
    return out if route else out[0]


def _swiglu_step(xn, wg_ref, wu_ref, wd_ref):
    g = _dot(xn, wg_ref[...].astype(BF16))
    u = _dot(xn, wu_ref[...].astype(BF16))
    a = (g * jax.nn.sigmoid(g) * u).astype(BF16)
    return _dot(a, wd_ref[...].astype(BF16))


def _ffn_kernel(x_ref, g_ref, wg_ref, wu_ref, wd_ref, o_ref, xn_ref, acc_ref):
    j = pl.program_id(1)

    @pl.when(j == 0)
    def _():
        xn_ref[...] = _rms(x_ref[...], g_ref[...]).astype(BF16)
        acc_ref[...] = jnp.zeros_like(acc_ref)

    acc_ref[...] += _swiglu_step(xn_ref[...], wg_ref, wu_ref, wd_ref)

    @pl.when(j == pl.num_programs(1) - 1)
    def _():
        o_ref[...] = x_ref[...] + acc_ref[...]


def dense_ffn(x, gain, wg, wu, wd, layer, tm=1024, tf=512):
    m, d = x.shape
    f = wg.shape[2]
    tm, tf = min(tm, m), min(tf, f)
    return pl.pallas_call(
        _ffn_kernel,
        grid=(m // tm, f // tf),
        in_specs=[pl.BlockSpec((tm, d), lambda i, j: (i, 0)),
                  pl.BlockSpec((1, d), lambda i, j: (0, 0)),
                  pl.BlockSpec((None, d, tf), lambda i, j: (layer, 0, j)),
                  pl.BlockSpec((None, d, tf), lambda i, j: (layer, 0, j)),
                  pl.BlockSpec((None, tf, d), lambda i, j: (layer, j, 0))],
        out_specs=pl.BlockSpec((tm, d), lambda i, j: (i, 0)),
        out_shape=jax.ShapeDtypeStruct((m, d), F32),
        scratch_shapes=[pltpu.VMEM((tm, d), BF16), pltpu.VMEM((tm, d), F32)],
        compiler_params=_params("parallel", "arbitrary"),
        name="dense_ffn",
    )(x, gain.reshape(1, d), wg, wu, wd)


def _retention_kernel(q_ref, k_ref, v_ref, g_ref, dec_ref, qd_ref, kd_ref, gn_ref, o_ref, r_ref,
                      *, heads, dk, dv, chunk_decay):
    @pl.when(pl.program_id(1) == 0)
    def _():
        r_ref[...] = jnp.zeros_like(r_ref)

    for h in range(heads):
        qh = q_ref[:, h * dk:(h + 1) * dk]
        kh = k_ref[:, h * dk:(h + 1) * dk]
        vh = v_ref[:, h * dv:(h + 1) * dv]
        s = _dot_nt(qh, kh) * dec_ref[h]
        y = _dot(s.astype(BF16), vh)
        y = y + _dot(qh, r_ref[h].astype(BF16)) * qd_ref[h]
        kd = (kh.astype(F32) * kd_ref[h]).astype(BF16)
        r_ref[h] = r_ref[h] * chunk_decay[h] + _dot_tn(kd, vh)
        mu = jnp.mean(y, axis=-1, keepdims=True)
        yc = y - mu
        var = jnp.mean(yc * yc, axis=-1, keepdims=True)
        yn = yc * lax.rsqrt(var + NORM_EPS) * gn_ref[:, h * dv:(h + 1) * dv]
        gate = g_ref[:, h * dv:(h + 1) * dv].astype(F32)
        o_ref[:, h * dv:(h + 1) * dv] = (yn * (gate * jax.nn.sigmoid(gate))).astype(o_ref.dtype)


def retention_core(proj, gn_gain, batch, seq):
    d = proj.shape[1] // 6
    heads = RET_HEADS
    dk, dv = d // heads, 2 * d // heads
    c = math.gcd(seq, RET_CHUNK)
    nc = seq // c
    log_gamma = np.log1p(-np.exp2(-5.0 - np.arange(heads, dtype=np.float64)))
    idx = np.arange(c, dtype=np.float64)
    diff = idx[:, None] - idx[None, :]
    scale = dk ** -0.5
    decay = np.where(diff >= 0, np.exp(log_gamma[:, None, None] * np.maximum(diff, 0.0)), 0.0) * scale
    q_decay = np.exp(log_gamma[:, None] * (idx + 1.0))[:, :, None]
    k_decay = np.exp(log_gamma[:, None] * (c - 1.0 - idx))[:, :, None] * scale
    chunk_decay = tuple(float(v) for v in np.exp(log_gamma * c))
    kern = functools.partial(_retention_kernel, heads=heads, dk=dk, dv=dv, chunk_decay=chunk_decay)
    row = lambda b, n: b * nc + n
    return pl.pallas_call(
        kern,
        grid=(batch, nc),
        in_specs=[pl.BlockSpec((c, d), lambda b, n: (row(b, n), 0)),
                  pl.BlockSpec((c, d), lambda b, n: (row(b, n), 1)),
                  pl.BlockSpec((c, 2 * d), lambda b, n: (row(b, n), 1)),
                  pl.BlockSpec((c, 2 * d), lambda b, n: (row(b, n), 2)),
                  pl.BlockSpec((heads, c, c), lambda b, n: (0, 0, 0)),
                  pl.BlockSpec((heads, c, 1), lambda b, n: (0, 0, 0)),
                  pl.BlockSpec((heads, c, 1), lambda b, n: (0, 0, 0)),
                  pl.BlockSpec((1, 2 * d), lambda b, n: (0, 0))],
        out_specs=pl.BlockSpec((c, 2 * d), lambda b, n: (row(b, n), 0)),
        out_shape=jax.ShapeDtypeStruct((batch * seq, 2 * d), BF16),
        scratch_shapes=[pltpu.VMEM((heads, dk, dv), F32)],
        compiler_params=_params("parallel", "arbitrary"),
        name="retention_core",
    )(proj, proj, proj, proj, jnp.asarray(decay, F32), jnp.asarray(q_decay, F32),
      jnp.asarray(k_decay, F32), gn_gain.reshape(1, 2 * d).astype(F32))


def _gelu_tanh(x):
    return 0.5 * x * (1.0 + jnp.tanh(math.sqrt(2.0 / math.pi) * (x + 0.044715 * (x * x * x))))


def _s5_expand(tc_ref, sc_ref, qc_ref, es_ref, eq_ref, m1_ref, m2_ref, *, sub, ga, half):
    width = sub * LANES
    n_ch = LANES // ga
    p = half // ga

    def keep_own_group(x, row_group, col_group):
        return jnp.where(row_group == col_group, x, 0.0).astype(BF16)

    lag = _dot(tc_ref[0], eq_ref[...])
    ra = lax.broadcasted_iota(jnp.int32, lag.shape, 0) // n_ch
    cb = (lax.broadcasted_iota(jnp.int32, lag.shape, 1) % LANES) // n_ch
    lag = keep_own_group(lag, ra, cb)
    for s in range(sub):
        rs = slice(s * LANES, (s + 1) * LANES)
        if s:
            m1_ref[rs, 0:s * LANES] = jnp.zeros((LANES, s * LANES), BF16)
        m1_ref[rs, s * LANES:width] = lag[:, 0:(sub - s) * LANES]
        sx = _dot(sc_ref[0, rs, :], es_ref[...])
        ra = lax.broadcasted_iota(jnp.int32, sx.shape, 0) // n_ch
        cb = (lax.broadcasted_iota(jnp.int32, sx.shape, 1) % half) // p
        m1_ref[rs, width:width + 2 * half] = keep_own_group(sx, ra, cb)
    for r in range(2 * ga):
        qx = _dot(qc_ref[0, r * p:(r + 1) * p, :], eq_ref[...])
        cb = (lax.broadcasted_iota(jnp.int32, qx.shape, 1) % LANES) // n_ch
        m2_ref[r * p:(r + 1) * p, :] = keep_own_group(qx, r % ga, cb)


def _s5_kernel(u_ref, d_ref, tc_ref, sc_ref, qc_ref, es_ref, eq_ref, al_ref, o_ref,
               m1_ref, m2_ref, xl_ref, xp_ref, st_ref, *, sub, rows, steps_per_seq, ga):
    half = st_ref.shape[1] // 2
    width = sub * LANES

    @pl.when(pl.program_id(1) == 0)
    def _():
        _s5_expand(tc_ref, sc_ref, qc_ref, es_ref, eq_ref, m1_ref, m2_ref, sub=sub, ga=ga, half=half)

    @pl.when(pl.program_id(1) % steps_per_seq == 0)
    def _():
        st_ref[...] = jnp.zeros_like(st_ref)

    us = [u_ref[pl.ds(t, rows, stride=sub), :] for t in range(sub)]
    ucat = jnp.concatenate([u.astype(BF16) for u in us], axis=1)
    y_all = _dot(ucat, m1_ref[...])
    xl_ref[...] = y_all[:, width:]

    a_re = al_ref[0, 0:1, :]
    a_im = al_ref[0, 1:2, :]

    def step(c, carry):
        xr, xi = carry
        xp_ref[pl.ds(c, 1), 0:half] = xr
        xp_ref[pl.ds(c, 1), half:2 * half] = xi
        lr = xl_ref[pl.ds(c, 1), 0:half]
        li = xl_ref[pl.ds(c, 1), half:2 * half]
        return a_re * xr - a_im * xi + lr, a_re * xi + a_im * xr + li

    xr, xi = lax.fori_loop(0, rows, step, (st_ref[0:1, 0:half], st_ref[0:1, half:2 * half]))
    st_ref[0:1, 0:half] = xr
    st_ref[0:1, half:2 * half] = xi

    y_cross = _dot(xp_ref[...].astype(BF16), m2_ref[...])
    skip = d_ref[...]
    for t in range(sub):
        y = y_all[:, t * LANES:(t + 1) * LANES] + y_cross[:, t * LANES:(t + 1) * LANES] + skip * us[t]
        o_ref[pl.ds(t, rows, stride=sub), :] = _gelu_tanh(y)


def _s5_operators(a_re, a_im, log_step, b_re, b_im, c_re, c_im, sub):
    g, p = a_re.shape
    n = b_re.shape[2]
    ga = S5_LANE_GROUPS
    j = g // ga
    dt = jnp.exp(log_step.astype(F32))[:, None]
    ar, ai = a_re.astype(F32), a_im.astype(F32)
    mag = jnp.exp(ar * dt)
    abar_re = mag * jnp.cos(ai * dt)
    abar_im = mag * jnp.sin(ai * dt)
    den = ar * ar + ai * ai
    nr, ni = abar_re - 1.0, abar_im
    f_re = (nr * ar + ni * ai) / den
    f_im = (ni * ar - nr * ai) / den
    br, bi = b_re.astype(F32), b_im.astype(F32)
    bb_re = f_re[..., None] * br - f_im[..., None] * bi
    bb_im = f_re[..., None] * bi + f_im[..., None] * br
    cr, ci = c_re.astype(F32), c_im.astype(F32)
    tau = jnp.arange(sub + 1, dtype=F32)[:, None, None]
    pw_mag = jnp.exp(tau * (ar * dt)[None])
    pw_re = pw_mag * jnp.cos(tau * (ai * dt)[None])
    pw_im = pw_mag * jnp.sin(tau * (ai * dt)[None])
    hp = lax.Precision.HIGHEST
    ab_re = pw_re[..., None] * bb_re[None] - pw_im[..., None] * bb_im[None]
    ab_im = pw_re[..., None] * bb_im[None] + pw_im[..., None] * bb_re[None]
    lag = (jnp.einsum('tgpm,gnp->tgmn', ab_re[:sub], cr, precision=hp)
           - jnp.einsum('tgpm,gnp->tgmn', ab_im[:sub], ci, precision=hp))
    tc = lag.reshape(sub, j, ga, n, n).transpose(1, 2, 3, 0, 4).reshape(j, ga * n, sub * n)
    rev = sub - 1 - jnp.arange(sub)
    sc = jnp.stack([ab_re[rev], ab_im[rev]], axis=0).reshape(2, sub, j, ga, p, n)
    sc = sc.transpose(2, 1, 3, 5, 0, 4).reshape(j, sub * ga * n, 2 * p)
    q_re = cr[None] * pw_re[1:][:, :, None, :] - ci[None] * pw_im[1:][:, :, None, :]
    q_im = -(cr[None] * pw_im[1:][:, :, None, :] + ci[None] * pw_re[1:][:, :, None, :])
    qc = jnp.stack([q_re, q_im], axis=0).reshape(2, sub, j, ga, n, p)
    qc = qc.transpose(2, 0, 3, 5, 1, 4).reshape(j, 2 * ga * p, sub * n)
    al = jnp.stack([pw_re[sub].reshape(j, ga * p), pw_im[sub].reshape(j, ga * p)], axis=1)
    return tc.astype(BF16), sc.astype(BF16), qc.astype(BF16), al


def _s5_spread_matrices(sub, ga, n, p):
    half = ga * p
    r = np.arange(2 * p)
    c = np.arange(2 * half)
    es = (r[:, None] // p == c[None, :] // half) & (r[:, None] % p == c[None, :] % p)
    r = np.arange(sub * n)
    c = np.arange(sub * ga * n)
    eq = (r[:, None] // n == c[None, :] // (ga * n)) & (r[:, None] % n == c[None, :] % n)
    return jnp.asarray(es, BF16), jnp.asarray(eq, BF16)


def s5_core(u, d_skip, ops, batch, seq, sub):
    tc, sc, qc, al = ops
    n_tok, w = u.shape
    j = tc.shape[0]
    half = al.shape[2]
    ga = S5_LANE_GROUPS
    es, eq = _s5_spread_matrices(sub, ga, LANES // ga, half // ga)
    rows = min(S5_ROWS, seq // sub)
    step_tok = rows * sub
    steps_per_seq = seq // step_tok
    width = sub * LANES
    kern = functools.partial(_s5_kernel, sub=sub, rows=rows, steps_per_seq=steps_per_seq, ga=ga)
    tile = lambda a: pl.BlockSpec((1,) + a.shape[1:], lambda jj, i: (jj, 0, 0))
    whole = lambda a: pl.BlockSpec(a.shape, lambda jj, i: (0, 0))
    return pl.pallas_call(
        kern,
        grid=(j, n_tok // step_tok),
        in_specs=[pl.BlockSpec((step_tok, LANES), lambda jj, i: (i, jj)),
                  pl.BlockSpec((1, LANES), lambda jj, i: (0, jj)),
                  tile(tc), tile(sc), tile(qc), whole(es), whole(eq), tile(al)],
        out_specs=pl.BlockSpec((step_tok, LANES), lambda jj, i: (i, jj)),
        out_shape=jax.ShapeDtypeStruct((n_tok, w), F32),
        scratch_shapes=[pltpu.VMEM((width, width + 2 * half), BF16), pltpu.VMEM((2 * half, width), BF16),
                        pltpu.VMEM((rows, 2 * half), F32), pltpu.VMEM((rows, 2 * half), F32),
                        pltpu.VMEM((8, 2 * half), F32)],
        compiler_params=_params("parallel", "arbitrary"),
        name="s5_core",
    )(u, d_skip.reshape(1, w).astype(F32), tc, sc, qc, es, eq, al)


MOBA_AUX_CONSTS = 6


def _moba_kernel(cst_ref, q_ref, k_ref, v_ref, o_ref, km_ref, ka_ref, vt_ref, sa_ref, sb_ref, acc_ref,
                 *, blk, topk, scale, group, qblocks):
    h = pl.program_id(1)
    i = pl.program_id(2)
    first = i * qblocks
    nb = k_ref.shape[0] // blk
    hd = q_ref.shape[1]
    aux = ka_ref.shape[2] - hd
    nc = MOBA_AUX_CONSTS

    @pl.when(i == 0)
    def _():
        km_ref[...] = jnp.zeros_like(km_ref)
        jl = lax.broadcasted_iota(jnp.int32, (blk, aux), 0)
        lane = lax.broadcasted_iota(jnp.int32, (blk, aux), 1)

        def fill(n, c):
            st = pl.multiple_of(n * blk, blk)
            kb = k_ref[pl.ds(st, blk), :]
            km_ref[pl.ds(n + nc, 1), :] = jnp.mean(kb.astype(F32), axis=0, keepdims=True)
            pat = jnp.where(lane < nc // 2, n, jnp.where(lane < nc, jl, (lane - nc == n).astype(jnp.int32)))
            ka_ref[n, :, 0:hd] = kb
            ka_ref[n, :, hd:hd + aux] = pat.astype(F32).astype(BF16)
            vt_ref[n] = v_ref[pl.ds(st, blk), :].astype(F32).T.astype(BF16)
            return c

        lax.fori_loop(0, nb, fill, 0)

    q = q_ref[...]
    rest = km_ref[...]
    gate = None
    for _ in range(3):
        piece = rest.astype(BF16)
        rest = rest - piece.astype(F32)
        part = _dot_nt(q, piece)
        gate = part if gate is None else gate + part
    lane = lax.broadcasted_iota(jnp.int32, gate.shape, 1)
    own = first + lax.broadcasted_iota(jnp.int32, gate.shape, 0) // blk
    gate = jnp.where(jnp.logical_and(lane >= nc, lane - nc < own), gate, NEG_INF)
    chosen = jnp.zeros(gate.shape, jnp.bool_)
    lane_f = lane.astype(F32)
    for _ in range(topk):
        mx = jnp.max(gate, axis=1, keepdims=True)
        idx = jnp.min(jnp.where(gate == mx, lane_f, float(2 ** 20)), axis=1, keepdims=True)
        hit = lane_f == idx
        chosen = jnp.logical_or(chosen, jnp.logical_and(hit, mx > 0.5 * NEG_INF))
        gate = jnp.where(hit, NEG_INF, gate)

    qx = jnp.where(chosen, 0.0, NEG_INF)
    for c in range(nc):
        qx = jnp.where(lane == c, cst_ref[h, c], qx)
    qa = jnp.concatenate([q, qx.astype(BF16)], axis=1)
    a_full = cst_ref[h, 0] + cst_ref[h, 1] + cst_ref[h, 2]
    c_full = cst_ref[h, 3] + cst_ref[h, 4] + cst_ref[h, 5]

    jk = lax.broadcasted_iota(jnp.int32, (blk, blk), 0)
    jq = lax.broadcasted_iota(jnp.int32, (blk, blk), 1)
    c2 = scale * math.log2(math.e)
    m0, l0, acc0 = [], [], []
    for w in range(qblocks):
        bw = first + w
        st = _dot_nt(ka_ref[bw, :, 0:hd], q[w * blk:(w + 1) * blk])
        st = st + (c_full * jk.astype(F32) + a_full * bw.astype(F32))
        st = jnp.where(jk <= jq, st, NEG_INF)
        mw = jnp.max(st, axis=0, keepdims=True)
        p = jnp.exp2((st - mw) * c2)
        m0.append(mw)
        l0.append(jnp.sum(p, axis=0, keepdims=True))
        acc0.append(_dot(vt_ref[bw], p.astype(BF16)))
    m0, l0, acc0 = (jnp.concatenate(x, axis=1) for x in (m0, l0, acc0))

    n_groups = nb // group

    half = group // 2

    def scores(g, s_ref):
        tops = []
        for u in range(2):
            ka = ka_ref[pl.ds(g * group + u * half, half)].reshape(half * blk, hd + aux)
            st = _dot_nt(ka, qa)
            s_ref[u * half * blk:(u + 1) * half * blk, :] = st
            tops.append(jnp.max(st, axis=0, keepdims=True))
        return jnp.maximum(tops[0], tops[1])

    def softmax(st, top, m, l):
        m_new = jnp.maximum(m, top)
        alpha = jnp.exp2((m - m_new) * c2)
        p = jnp.exp2((st - m_new) * c2)
        return m_new, alpha * l + jnp.sum(p, axis=0, keepdims=True), alpha, p.astype(BF16)

    def weighted_values(g, pb):
        pv = _dot(vt_ref[g * group], pb[0:blk])
        for u in range(1, group):
            pv = pv + _dot(vt_ref[g * group + u], pb[u * blk:(u + 1) * blk])
        return pv

    top0 = scores(0, sa_ref)
    acc_ref[...] = acc0
    past = first + qblocks - 1

    def body(t, carry):
        m, l, top_a = carry
        g0 = 2 * t
        g1 = g0 + 1
        g2 = jnp.minimum(g0 + 2, n_groups - 1)
        top_b = scores(g1, sb_ref)
        m, l, alpha, pa = softmax(sa_ref[...], top_a, m, l)
        acc_ref[...] = alpha * acc_ref[...] + weighted_values(g0, pa)
        top_a = scores(g2, sa_ref)
        m, l, alpha, pb = softmax(sb_ref[...], top_b, m, l)
        acc_ref[...] = alpha * acc_ref[...] + weighted_values(g1, pb)
        return m, l, top_a

    units = (past + group - 1) // group
    m, l, top_a = lax.fori_loop(0, units // 2, body, (m0, l0, top0))

    def odd_group(m, l):
        m, l, alpha, pa = softmax(sa_ref[...], top_a, m, l)
        acc_ref[...] = alpha * acc_ref[...] + weighted_values(units - 1, pa)
        return m, l

    m, l = lax.cond(units % 2 == 1, odd_group, lambda m, l: (m, l), m, l)
    o_ref[...] = (acc_ref[...] / l).T.astype(o_ref.dtype)


def moba_core(qkv, batch, seq):
    d = qkv.shape[1] // 3
    heads = MOBA_HEADS
    hd = d // heads
    blk = MOBA_BLOCK
    assert seq % blk == 0
    nb = seq // blk
    topk = max(1, min(MOBA_TOPK, nb - 1))
    group = math.gcd(nb, MOBA_GROUP)
    aux = LANES
    assert MOBA_AUX_CONSTS + nb <= aux and (nb // group) % 2 == 0
    scale = hd ** -0.5
    slopes = np.exp2(-8.0 * (np.arange(heads, dtype=np.float64) + 1.0) / heads)
    consts = []
    for val in (slopes * blk / scale, slopes / scale):
        rest = jnp.asarray(val, F32)
        for _ in range(MOBA_AUX_CONSTS // 2):
            piece = rest.astype(BF16).astype(F32)
            consts.append(piece)
            rest = rest - piece
    cst = jnp.stack(consts, axis=1)
    qblocks = math.gcd(nb, MOBA_QUERY_BLOCKS)
    nq = nb // qblocks
    kern = functools.partial(_moba_kernel, blk=blk, topk=topk, scale=scale, group=group, qblocks=qblocks)
    return pl.pallas_call(
        kern,
        grid=(batch, heads, nq),
        in_specs=[pl.BlockSpec(memory_space=pltpu.SMEM),
                  pl.BlockSpec((qblocks * blk, hd), lambda b, h, i: (b * nq + i, h)),
                  pl.BlockSpec((seq, hd), lambda b, h, i: (b, heads + h)),
                  pl.BlockSpec((seq, hd), lambda b, h, i: (b, 2 * heads + h))],
        out_specs=pl.BlockSpec((qblocks * blk, hd), lambda b, h, i: (b * nq + i, h)),
        out_shape=jax.ShapeDtypeStruct((batch * seq, d), BF16),
        scratch_shapes=[pltpu.VMEM((aux, hd), F32), pltpu.VMEM((nb, blk, hd + aux), BF16),
                        pltpu.VMEM((nb, hd, blk), BF16),
                        pltpu.VMEM((group * blk, qblocks * blk), F32),
                        pltpu.VMEM((group * blk, qblocks * blk), F32),
                        pltpu.VMEM((hd, qblocks * blk), F32)],
        compiler_params=_params("parallel", "parallel", "arbitrary"),
        name="moba_core",
    )(cst, qkv, qkv, qkv)


def _to_token_tiles(ref, x):
    pieces = x.shape[1] // LANES
    for s in range(pieces):
        ref[pl.ds(s, x.shape[0], stride=pieces), :] = x[:, s * LANES:(s + 1) * LANES]


def _from_token_tiles(ref, tokens, pieces):
    return [ref[pl.ds(s, tokens, stride=pieces), :] for s in range(pieces)]


def _route_tables(logits, rows):
    n, e = logits.shape
    n_pairs = n * TOP_K
    i32 = jnp.int32
    top_logits, top_idx = lax.top_k(logits, TOP_K)
    gates = jax.nn.softmax(top_logits, axis=-1).reshape(-1)
    e_flat = top_idx.reshape(-1).astype(i32)
    e_sorted, pair_sorted, gate_sorted = lax.sort((e_flat, jnp.arange(n_pairs, dtype=i32), gates),
                                                  num_keys=1, is_stable=True)
    experts = jnp.arange(e, dtype=i32)
    start = jnp.searchsorted(e_sorted, experts, side='left', method='compare_all').astype(i32)
    counts = jnp.searchsorted(e_sorted, experts, side='right', method='compare_all').astype(i32) - start
    padded = ((counts + rows - 1) // rows) * rows
    pend = jnp.cumsum(padded)
    pstart = pend - padded
    n_blocks = -(-n_pairs // rows) + e
    blk0 = jnp.arange(n_blocks, dtype=i32) * rows
    block_e = jnp.minimum(jnp.searchsorted(pend, blk0, side='right', method='compare_all'), e - 1).astype(i32)
    first = blk0 - pstart[block_e]
    live = jnp.clip(counts[block_e] - first, 0, rows)
    off = jnp.clip(start[block_e] + first, 0, n_pairs)
    r = jnp.arange(rows, dtype=i32)[None, :]
    idx = jnp.minimum(off[:, None] + r, n_pairs - 1)
    pair_w, gate_w = pair_sorted[idx], gate_sorted[idx]
    valid = r < live[:, None]
    tok = pair_w // TOP_K
    slot_tok = jnp.where(valid, tok, 0).astype(i32)
    slot_dst = jnp.where(valid, (pair_w % TOP_K) * n + tok, TOP_K * n + r).astype(i32)
    slot_gate = jnp.where(valid, gate_w, 0.0).astype(F32)
    block_valid = (blk0 < pend[-1]).astype(i32)
    return (block_e, block_valid, slot_tok.reshape(n_blocks, 1, rows),
            slot_dst.reshape(n_blocks, 1, rows), slot_gate.reshape(n_blocks * rows, 1))


def _moe_kernel(be_ref, bv_ref, tok_ref, tok_next_ref, dst_ref, gate_ref, xn_hbm, wg_ref, wu_ref, wd_ref, y_hbm,
                xg_ref, xb_ref, acc_ref, yb_ref, sem_in, sem_out, *, rows):
    i = pl.program_id(0)
    j = pl.program_id(1)
    last_j = pl.num_programs(1) - 1
    valid = bv_ref[i] != 0
    nxt = jnp.minimum(i + 1, pl.num_programs(0) - 1)
    next_valid = jnp.logical_and(i + 1 < pl.num_programs(0), bv_ref[nxt] != 0)
    pieces = SUBLANES

    def tile(ref, t):
        return ref.at[pl.ds(pl.multiple_of(t * pieces, pieces), pieces)]

    def gather(table_ref, slot):
        def issue(r, c):
            pltpu.make_async_copy(tile(xn_hbm, table_ref[0, 0, r]), tile(xg_ref.at[slot], r), sem_in.at[slot]).start()
            return c

        lax.fori_loop(0, rows, issue, 0, unroll=MOE_ISSUE_UNROLL)

    def gather_wait(slot):
        pltpu.make_async_copy(xg_ref.at[slot], xg_ref.at[slot], sem_in.at[slot]).wait()

    def scatter():
        def issue(r, c):
            pltpu.make_async_copy(tile(yb_ref, r), tile(y_hbm, dst_ref[0, 0, r]), sem_out).start()
            return c

        lax.fori_loop(0, rows, issue, 0, unroll=MOE_ISSUE_UNROLL)

    def scatter_wait():
        pltpu.make_async_copy(yb_ref, yb_ref, sem_out).wait()

    @pl.when(jnp.logical_and(i == 0, j == 0))
    def _():
        yb_ref[...] = jnp.zeros_like(yb_ref)
        n_spare = rows * pieces
        spare = pltpu.make_async_copy(yb_ref, y_hbm.at[pl.ds(y_hbm.shape[0] - n_spare, n_spare)], sem_out)
        spare.start()
        spare.wait()

        @pl.when(valid)
        def _():
            gather(tok_ref, 0)

    @pl.when(jnp.logical_and(valid, j == 0))
    def _():
        for slot in range(2):
            @pl.when(i % 2 == slot)
            def _():
                gather_wait(slot)
                for s, piece in enumerate(_from_token_tiles(xg_ref.at[slot], rows, pieces)):
                    xb_ref[:, s * LANES:(s + 1) * LANES] = piece.astype(BF16)

                @pl.when(next_valid)
                def _():
                    gather(tok_next_ref, 1 - slot)

        acc_ref[...] = jnp.zeros_like(acc_ref)

    @pl.when(valid)
    def _():
        acc_ref[...] += _swiglu_step(xb_ref[...], wg_ref, wu_ref, wd_ref)

    @pl.when(jnp.logical_and(valid, j == last_j))
    def _():
        @pl.when(i > 0)
        def _():
            scatter_wait()

        _to_token_tiles(yb_ref, acc_ref[...] * gate_ref[...])
        scatter()

        @pl.when(jnp.logical_not(next_valid))
        def _():
            scatter_wait()


def moe_ffn(xn, tables, wg, wu, wd, layer, tf=512):
    d, f = wg.shape[2], wg.shape[3]
    n = xn.shape[0] // SUBLANES
    block_e, block_valid, slot_tok, slot_dst, slot_gate = tables
    n_blocks, _, rows = slot_tok.shape
    tf = min(tf, f)
    kern = functools.partial(_moe_kernel, rows=rows)
    grid_spec = pltpu.PrefetchScalarGridSpec(
        num_scalar_prefetch=2,
        grid=(n_blocks, f // tf),
        in_specs=[pl.BlockSpec((1, 1, rows), lambda i, j, be, bv: (i, 0, 0), memory_space=pltpu.SMEM),
                  pl.BlockSpec((1, 1, rows), lambda i, j, be, bv: (jnp.minimum(i + 1, n_blocks - 1), 0, 0),
                               memory_space=pltpu.SMEM),
                  pl.BlockSpec((1, 1, rows), lambda i, j, be, bv: (i, 0, 0), memory_space=pltpu.SMEM),
                  pl.BlockSpec((rows, 1), lambda i, j, be, bv: (i, 0)),
                  pl.BlockSpec(memory_space=pl.ANY),
                  pl.BlockSpec((None, None, d, tf), lambda i, j, be, bv: (layer, be[i], 0, j)),
                  pl.BlockSpec((None, None, d, tf), lambda i, j, be, bv: (layer, be[i], 0, j)),
                  pl.BlockSpec((None, None, tf, d), lambda i, j, be, bv: (layer, be[i], j, 0))],
        out_specs=pl.BlockSpec(memory_space=pl.ANY),
        scratch_shapes=[pltpu.VMEM((2, rows * SUBLANES, LANES), F32), pltpu.VMEM((rows, d), BF16),
                        pltpu.VMEM((rows, d), F32), pltpu.VMEM((rows * SUBLANES, LANES), F32),
                        pltpu.SemaphoreType.DMA((2,)), pltpu.SemaphoreType.DMA(())])
    return pl.pallas_call(
        kern,
        grid_spec=grid_spec,
        out_shape=jax.ShapeDtypeStruct(((TOP_K * n + rows) * SUBLANES, LANES), F32),
        compiler_params=_params("arbitrary", "arbitrary"),
        name="moe_ffn",
    )(block_e, block_valid, slot_tok, slot_tok, slot_dst, slot_gate, xn, wg, wu, wd)


def _combine_tiles(h_ref, y0_ref, y1_ref):
    tm, d = h_ref.shape
    pieces = d // LANES
    y0 = _from_token_tiles(y0_ref, tm, pieces)
    y1 = _from_token_tiles(y1_ref, tm, pieces)
    return jnp.concatenate([y0[s] + y1[s] for s in range(pieces)], axis=1) + h_ref[...]


def _combine_kernel(h_ref, y0_ref, y1_ref, g_ref, o_ref, *, norm):
    x = _combine_tiles(h_ref, y0_ref, y1_ref)
    o_ref[...] = _rms(x, g_ref[...]) if norm else x


def moe_combine(h, routed, final_gain=None, tm=512):
    n, d = h.shape
    tm = min(tm, n)
    nt = n // tm
    norm = final_gain is not None
    gain = final_gain if norm else jnp.ones((d,), F32)
    return pl.pallas_call(
        functools.partial(_combine_kernel, norm=norm),
        grid=(nt,),
        in_specs=[pl.BlockSpec((tm, d), lambda i: (i, 0)),
                  pl.BlockSpec((tm * SUBLANES, LANES), lambda i: (i, 0)),
                  pl.BlockSpec((tm * SUBLANES, LANES), lambda i: (nt + i, 0)),
                  pl.BlockSpec((1, d), lambda i: (0, 0))],
        out_specs=pl.BlockSpec((tm, d), lambda i: (i, 0)),
        out_shape=jax.ShapeDtypeStruct((n, d), F32),
        compiler_params=_params("parallel"),
        name="moe_combine",
    )(h, routed, routed, gain.reshape(1, d))


def _final_norm_kernel(h_ref, g_ref, o_ref):
    o_ref[...] = _rms(h_ref[...], g_ref[...])


def final_norm(h, gain, tm=512):
    n, d = h.shape
    tm = min(tm, n)
    return pl.pallas_call(
        _final_norm_kernel,
        grid=(n // tm,),
        in_specs=[pl.BlockSpec((tm, d), lambda i: (i, 0)), pl.BlockSpec((1, d), lambda i: (0, 0))],
        out_specs=pl.BlockSpec((tm, d), lambda i: (i, 0)),
        out_shape=jax.ShapeDtypeStruct((n, d), F32),
        compiler_params=_params("parallel"),
        name="final_norm",
    )(h, gain.reshape(1, d))


def kernel(x, norm_mix, norm_ffn, norm_final, ret_w_in, ret_gn, ret_w_out, s5_w_in, s5_a_re, s5_a_im, s5_log_step, s5_b_re, s5_b_im, s5_c_re, s5_c_im, s5_d, s5_w_out, moba_w_in, moba_w_out, ffn_w_gate, ffn_w_up, ffn_w_down, moe_router, moe_w_gate, moe_w_up, moe_w_down):
    batch, seq, d = x.shape
    depth = norm_mix.shape[0]
    n = batch * seq
    h = x.reshape(n, d).astype(F32)
    i_ret = i_s5 = i_moba = i_dense = i_moe = 0
    out = None
    for i in range(depth):
        mixer = i % N_MIXERS
        moe = i % 2 == 1
        route = (norm_ffn[i], moe_router, i_moe) if moe else None
        if mixer == 0:
            proj = norm_matmul(h, norm_mix[i], ret_w_in, i_ret, BF16)
            y = retention_core(proj, ret_gn[i_ret], batch, seq)
            res = out_projection(y, ret_w_out, i_ret, h, route=route)
            i_ret += 1
        elif mixer == 1:
            sub = math.gcd(seq, S5_SUBCHUNK)
            u = norm_matmul(h, norm_mix[i], s5_w_in, i_s5, F32)
            ops = _s5_operators(s5_a_re[i_s5], s5_a_im[i_s5], s5_log_step[i_s5], s5_b_re[i_s5],
                                s5_b_im[i_s5], s5_c_re[i_s5], s5_c_im[i_s5], sub)
            y = s5_core(u, s5_d[i_s5], ops, batch, seq, sub)
            res = out_projection(y, s5_w_out, i_s5, h, glu=True, route=route)
            i_s5 += 1
        else:
            qkv = norm_matmul(h, norm_mix[i], moba_w_in, i_moba, BF16)
            o = moba_core(qkv, batch, seq)
            res = out_projection(o, moba_w_out, i_moba, h, route=route)
            i_moba += 1
        last = i == depth - 1
        if not moe:
            h = dense_ffn(res, norm_ffn[i], ffn_w_gate, ffn_w_up, ffn_w_down, i_dense)
            i_dense += 1
            if last:
                out = final_norm(h, norm_final)
        else:
            h, xn, logits = res
            tables = _route_tables(logits, min(MOE_ROWS, n))
            routed = moe_ffn(xn, tables, moe_w_gate, moe_w_up, moe_w_down, i_moe)
            i_moe += 1
            h = moe_combine(h, routed, norm_final if last else None)
            if last:
                out = h
    return out.reshape(batch, seq, d).astype(x.dtype)
```
